```python
import math
import jax
import jax.numpy as jnp
from jax import lax
import numpy as np

D_MODEL = 2048
BATCH = 2
SEQ = 4096
DEPTH = 4
DEC_BATCH = 128
DEC_SEQ = 4
PAST_LEN = 8192
PAGE_SIZE = 128

RW_HEADS = 16
RW_HEAD_DIM = 64
RW_WIDTH = RW_HEADS * RW_HEAD_DIM
RW_DECAY_LORA = 64
RW_AAA_LORA = 64
RW_SHIFT_DIM = 3 * RW_WIDTH + RW_DECAY_LORA + RW_AAA_LORA

MLA_HEADS = 4
MLA_NOPE = 128
MLA_ROPE = 64
MLA_QK = MLA_NOPE + MLA_ROPE
MLA_V = 128
MLA_WIDTH = MLA_HEADS * MLA_V
MLA_Q_RANK = 384
MLA_KV_RANK = 256
MLA_IN = MLA_Q_RANK + MLA_KV_RANK + MLA_ROPE
ROPE_THETA = 10000.0
MLA_SCALE = MLA_QK ** -0.5

DF_HEADS = 4
DF_KV_HEADS = 1
DF_QK = 64
DF_V = 128
DF_WIDTH = DF_HEADS * DF_V
DF_KDIM = DF_KV_HEADS * 2 * DF_QK
DF_VDIM = DF_KV_HEADS * DF_V
DF_IN = DF_HEADS * 2 * DF_QK + DF_KDIM + DF_VDIM
DF_SCALE = DF_QK ** -0.5

D_MIX = RW_WIDTH + MLA_WIDTH + DF_WIDTH

OFF_RW = 0
OFF_RW_G = OFF_RW + RW_SHIFT_DIM
OFF_MLA = OFF_RW_G + RW_WIDTH
OFF_MLA_G = OFF_MLA + MLA_IN
OFF_DF = OFF_MLA_G + MLA_WIDTH
OFF_DF_G = OFF_DF + DF_IN
N_IN = OFF_DF_G + DF_WIDTH

ATTN_BLOCK = 128
NORM_EPS = 1e-6
GN_EPS = 64e-5
STATE_KEYS = ('mla_latent', 'mla_krope', 'diff_k', 'diff_v', 'rwkv_state', 'rwkv_shift')

kernel_name = "hybrid_rwkv7_mla_diffattn_step"


def rms_norm(x, g, eps=NORM_EPS):
    xf = x.astype(jnp.float32)
    y = xf * lax.rsqrt(jnp.mean(xf * xf, axis=-1, keepdims=True) + eps)
    return (y * g.astype(jnp.float32)).astype(x.dtype)


def rope(x, pos):
    half = x.shape[-1] // 2
    inv = ROPE_THETA ** (-jnp.arange(half, dtype=jnp.float32) / half)
    ang = pos.astype(jnp.float32)[:, None] * inv[None, :]
    cos = jnp.cos(ang)[None, :, None, :]
    sin = jnp.sin(ang)[None, :, None, :]
    xf = x.astype(jnp.float32)
    x1, x2 = xf[..., :half], xf[..., half:]
    return jnp.concatenate([x1 * cos - x2 * sin, x1 * sin + x2 * cos], axis=-1).astype(x.dtype)


def attend(q, k, v, q_pos, k_pos, scale):
    s = jnp.einsum('btgrd,bsgd->bgrts', q, k, preferred_element_type=jnp.float32) * scale
    mask = k_pos[None, :] <= q_pos[:, None]
    s = jnp.where(mask, s, -jnp.inf)
    p = jax.nn.softmax(s, axis=-1)
    return jnp.einsum('bgrts,bsgd->btgrd', p.astype(v.dtype), v)


def blocked_attend(q, k, v, scale):
    B, T = q.shape[0], q.shape[1]
    nb = T // ATTN_BLOCK
    qb = jnp.swapaxes(q.reshape((B, nb, ATTN_BLOCK) + q.shape[2:]), 0, 1)
    pos_b = jnp.arange(T).reshape(nb, ATTN_BLOCK)
    k_pos = jnp.arange(k.shape[1])
    out = lax.map(lambda a: attend(a[0], k, v, a[1], k_pos, scale), (qb, pos_b))
    out = jnp.swapaxes(out, 0, 1)
    return out.reshape((B, T) + out.shape[3:])


def paged_rows(pool, page_row):
    return pool[page_row].reshape(page_row.shape[0] * pool.shape[1], pool.shape[2])


def rwkv_time_mix(z, shift_prev, s0, mu, w0, w_up, a0, a_up, k_k, k_a, r_k, gn_g, gn_b):
    B, T, _ = z.shape
    f32 = jnp.float32
    prev = jnp.concatenate([shift_prev[:, None, :].astype(z.dtype), z[:, :-1]], axis=1)
    zs = z + (prev - z) * mu
    r, k, v, wd, ad = jnp.split(zs, np.cumsum([RW_WIDTH, RW_WIDTH, RW_WIDTH, RW_DECAY_LORA]), axis=-1)
    w_log = -jax.nn.softplus(-(w0 + jnp.tanh(wd) @ w_up).astype(f32)) - 0.5
    decay = jnp.exp(-jnp.exp(w_log))
    a = jax.nn.sigmoid((a0 + ad @ a_up).astype(f32))
    hd = (B, T, RW_HEADS, RW_HEAD_DIM)
    r, k, v, decay, a = [t.astype(f32).reshape(hd) for t in (r, k, v, decay, a)]
    kk = k * k_k.astype(f32).reshape(RW_HEADS, RW_HEAD_DIM)
    kk = kk / jnp.maximum(jnp.sqrt(jnp.sum(kk * kk, axis=-1, keepdims=True)), 1e-12)
    k = k * (1.0 + (a - 1.0) * k_a.astype(f32).reshape(RW_HEADS, RW_HEAD_DIM))

    def step(S, inp):
        r_t, w_t, k_t, v_t, kk_t, a_t = inp
        S = (S * w_t[:, :, None, :]
             - jnp.einsum('bhij,bhj->bhi', S, kk_t)[..., None] * (kk_t * a_t)[:, :, None, :]
             + v_t[..., None] * k_t[:, :, None, :])
        return S, jnp.einsum('bhij,bhj->bhi', S, r_t)

    xs = tuple(jnp.moveaxis(t, 1, 0) for t in (r, decay, k, v, kk, a))
    s_fin, o = lax.scan(step, s0.astype(f32), xs)
    o = jnp.moveaxis(o, 0, 1)
    mean = jnp.mean(o, axis=-1, keepdims=True)
    var = jnp.mean(jnp.square(o - mean), axis=-1, keepdims=True)
    o = ((o - mean) * lax.rsqrt(var + GN_EPS)).reshape(B, T, RW_WIDTH)
    o = o * gn_g.astype(f32) + gn_b.astype(f32)
    bonus = jnp.sum(r * k * r_k.astype(f32), axis=-1, keepdims=True) * v
    o = o + bonus.reshape(B, T, RW_WIDTH)
    return o.astype(z.dtype), z[:, -1], s_fin.astype(s0.dtype)


def mla_project(z, pos, q_norm_g, w_uq, kv_norm_g, qn_g):
    B, T, _ = z.shape
    cq = rms_norm(z[..., :MLA_Q_RANK], q_norm_g)
    q = (cq @ w_uq).reshape(B, T, MLA_HEADS, MLA_QK)
    q = jnp.concatenate([q[..., :MLA_NOPE], rope(q[..., MLA_NOPE:], pos)], axis=-1)
    q = rms_norm(q, qn_g)
    c = rms_norm(z[..., MLA_Q_RANK:MLA_Q_RANK + MLA_KV_RANK], kv_norm_g)
    kr = rope(z[..., MLA_Q_RANK + MLA_KV_RANK:][:, :, None, :], pos)[:, :, 0, :]
    return q, c, kr


def mla_keys_values(c, kr, w_ukv, kn_g):
    B, S, _ = c.shape
    kv = (c @ w_ukv).reshape(B, S, MLA_HEADS, MLA_NOPE + MLA_V)
    k = jnp.concatenate([kv[..., :MLA_NOPE],
                         jnp.broadcast_to(kr[:, :, None, :], (B, S, MLA_HEADS, MLA_ROPE))], axis=-1)
    return rms_norm(k, kn_g), kv[..., MLA_NOPE:]


def mla_paged_attend(q, c, kr, pool_c, pool_kr, page_table, q_pos, w_ukv, kn_g):
    n_past = page_table.shape[1] * PAGE_SIZE
    k_pos = jnp.arange(n_past + q.shape[1])

    def one(args):
        pt, q1, c1, kr1 = args
        c_all = jnp.concatenate([paged_rows(pool_c, pt).astype(c1.dtype), c1], axis=0)[None]
        kr_all = jnp.concatenate([paged_rows(pool_kr, pt).astype(kr1.dtype), kr1], axis=0)[None]
        k, v = mla_keys_values(c_all, kr_all, w_ukv, kn_g)
        return attend(q1[None, :, :, None, :], k, v, q_pos, k_pos, MLA_SCALE)[0]

    return lax.map(one, (page_table, q, c, kr))


def diff_project(z, qn_g, kn_g):
    B, T, _ = z.shape
    nq = DF_HEADS * 2 * DF_QK
    q = rms_norm(z[..., :nq].reshape(B, T, DF_HEADS, 2, DF_QK), qn_g)
    k = rms_norm(z[..., nq:nq + DF_KDIM].reshape(B, T, DF_KV_HEADS, 2, DF_QK), kn_g)
    v = z[..., nq + DF_KDIM:]
    return q, k.reshape(B, T, DF_KDIM), v


def diff_arrange_q(q):
    B, T = q.shape[0], q.shape[1]
    rep = DF_HEADS // DF_KV_HEADS
    q = q.reshape(B, T, DF_KV_HEADS, rep, 2, DF_QK).transpose(0, 1, 4, 2, 3, 5)
    return q.reshape(B, T, 2 * DF_KV_HEADS, rep, DF_QK)


def diff_arrange_kv(k_rows, v_rows):
    B, S = k_rows.shape[0], k_rows.shape[1]
    k = k_rows.reshape(B, S, DF_KV_HEADS, 2, DF_QK).transpose(0, 1, 3, 2, 4)
    k = k.reshape(B, S, 2 * DF_KV_HEADS, DF_QK)
    v = jnp.broadcast_to(v_rows.reshape(B, S, 1, DF_KV_HEADS, DF_V), (B, S, 2, DF_KV_HEADS, DF_V))
    return k, v.reshape(B, S, 2 * DF_KV_HEADS, DF_V)


def diff_merge(o, lam):
    B, T = o.shape[0], o.shape[1]
    o = o.reshape(B, T, 2, DF_HEADS, DF_V)
    return o[:, :, 0] - lam * o[:, :, 1]


def diff_lambda(lq1, lk1, lq2, lk2, lam_init):
    f = jnp.float32
    return (jnp.exp(jnp.sum(lq1.astype(f) * lk1.astype(f)))
            - jnp.exp(jnp.sum(lq2.astype(f) * lk2.astype(f))) + lam_init)


def diff_paged_attend(qg, k_new, v_new, pool_k, pool_v, page_table, q_pos):
    n_past = page_table.shape[1] * PAGE_SIZE
    k_pos = jnp.arange(n_past + qg.shape[1])

    def one(args):
        pt, q1, k1, v1 = args
        k_all = jnp.concatenate([paged_rows(pool_k, pt).astype(k1.dtype), k1], axis=0)[None]
        v_all = jnp.concatenate([paged_rows(pool_v, pt).astype(v1.dtype), v1], axis=0)[None]
        kg, vg = diff_arrange_kv(k_all, v_all)
        return attend(q1[None], kg, vg, q_pos, k_pos, DF_SCALE)[0]

    return lax.map(one, (page_table, qg, k_new, v_new))


def trunk(x, pos, P, past):
    B, T, _ = x.shape
    new = {name: [] for name in STATE_KEYS}
    for i in range(DEPTH):
        h = rms_norm(x, P['ln_g'][i])
        z = h @ P['w_in'][i]
        z_rw, g_rw = z[..., OFF_RW:OFF_RW_G], z[..., OFF_RW_G:OFF_MLA]
        z_mla, g_mla = z[..., OFF_MLA:OFF_MLA_G], z[..., OFF_MLA_G:OFF_DF]
        z_df, g_df = z[..., OFF_DF:OFF_DF_G], z[..., OFF_DF_G:N_IN]

        if past is None:
            shift0 = jnp.zeros((B, RW_SHIFT_DIM), z.dtype)
            s0 = jnp.zeros((B, RW_HEADS, RW_HEAD_DIM, RW_HEAD_DIM), jnp.float32)
        else:
            shift0, s0 = past['rwkv_shift'][i], past['rwkv_state'][i]
        y_rw, shift1, s1 = rwkv_time_mix(
            z_rw, shift0, s0, P['rw_mu'][i], P['rw_w0'][i], P['rw_w_up'][i], P['rw_a0'][i],
            P['rw_a_up'][i], P['rw_k_k'][i], P['rw_k_a'][i], P['rw_r_k'][i],
            P['rw_gn_g'][i], P['rw_gn_b'][i])

        q, c, kr = mla_project(z_mla, pos, P['mla_q_norm_g'][i], P['mla_w_uq'][i],
                               P['mla_kv_norm_g'][i], P['mla_qn_g'][i])
        if past is None:
            k, v = mla_keys_values(c, kr, P['mla_w_ukv'][i], P['mla_kn_g'][i])
            o_mla = blocked_attend(q[:, :, :, None, :], k, v, MLA_SCALE)
        else:
            o_mla = mla_paged_attend(q, c, kr, past['mla_latent'][i], past['mla_krope'][i],
                                     past['page_table'], pos, P['mla_w_ukv'][i], P['mla_kn_g'][i])
        o_mla = o_mla.reshape(B, T, MLA_WIDTH)

        qd, kd, vd = diff_project(z_df, P['df_qn_g'][i], P['df_kn_g'][i])
        qg = diff_arrange_q(qd)
        if past is None:
            kg, vg = diff_arrange_kv(kd, vd)
            o = blocked_attend(qg, kg, vg, DF_SCALE)
        else:
            o = diff_paged_attend(qg, kd, vd, past['diff_k'][i], past['diff_v'][i],
                                  past['page_table'], pos)
        lam_init = 0.8 - 0.6 * math.exp(-0.3 * i)
        lam = diff_lambda(P['df_lq1'][i], P['df_lk1'][i], P['df_lq2'][i], P['df_lk2'][i], lam_init)
        o_df = rms_norm(diff_merge(o, lam.astype(o.dtype)), P['df_subln_g'][i]) * (1.0 - lam_init)
        o_df = o_df.reshape(B, T, DF_WIDTH)

        mix = jnp.concatenate([y_rw * jax.nn.silu(g_rw), o_mla * jax.nn.silu(g_mla),
                               o_df * jax.nn.silu(g_df)], axis=-1)
        x = x + mix @ P['w_out'][i]
        for name, val in zip(STATE_KEYS, (c, kr, kd, vd, s1, shift1)):
            new[name].append(val)
    return x, {name: jnp.stack(vals) for name, vals in new.items()}


def setup_inputs(seed: int = 0) -> dict:
    key = jax.random.key(seed)
    ks = iter(jax.random.split(key, 48))
    f32 = jnp.float32

    def nrm(shape, scale=1.0):
        return jax.random.normal(next(ks), shape, f32) * scale

    def gain(shape):
        return 1.0 + nrm(shape, 0.05)

    n_pages = PAST_LEN // PAGE_SIZE
    n_used = DEC_BATCH * n_pages
    n_pool = n_used + max(1, n_used // 4)

    x_prompt = nrm((BATCH, SEQ, D_MODEL))
    x_sample = nrm((DEC_BATCH, DEC_SEQ, D_MODEL))
    cache_mla_latent = nrm((DEPTH, n_pool, PAGE_SIZE, MLA_KV_RANK))
    cache_mla_krope = nrm((DEPTH, n_pool, PAGE_SIZE, MLA_ROPE))
    cache_diff_k = nrm((DEPTH, n_pool, PAGE_SIZE, DF_KDIM))
    cache_diff_v = nrm((DEPTH, n_pool, PAGE_SIZE, DF_VDIM))
    state_rwkv = nrm((DEPTH, DEC_BATCH, RW_HEADS, RW_HEAD_DIM, RW_HEAD_DIM), 0.3)
    state_rwkv_shift = nrm((DEPTH, DEC_BATCH, RW_SHIFT_DIM))
    perm = jax.random.permutation(next(ks), n_pool)
    page_table = perm[:n_used].reshape(DEC_BATCH, n_pages).astype(jnp.int32)

    return {
        'x_prompt': x_prompt,
        'x_sample': x_sample,
        'cache_mla_latent': cache_mla_latent,
        'cache_mla_krope': cache_mla_krope,
        'cache_diff_k': cache_diff_k,
        'cache_diff_v': cache_diff_v,
        'state_rwkv': state_rwkv,
        'state_rwkv_shift': state_rwkv_shift,
        'page_table': page_table,
        'ln_g': gain((DEPTH, D_MODEL)),
        'w_in': nrm((DEPTH, D_MODEL, N_IN), D_MODEL ** -0.5),
        'w_out': nrm((DEPTH, D_MIX, D_MODEL), 0.5 * D_MIX ** -0.5),
        'rw_mu': jax.random.uniform(next(ks), (DEPTH, RW_SHIFT_DIM), f32, 0.1, 0.9),
        'rw_w0': nrm((DEPTH, RW_WIDTH), 0.5),
        'rw_w_up': nrm((DEPTH, RW_DECAY_LORA, RW_WIDTH), 0.5 * RW_DECAY_LORA ** -0.5),
        'rw_a0': nrm((DEPTH, RW_WIDTH), 0.5),
        'rw_a_up': nrm((DEPTH, RW_AAA_LORA, RW_WIDTH), 0.5 * RW_AAA_LORA ** -0.5),
        'rw_k_k': 0.85 + nrm((DEPTH, RW_WIDTH), 0.05),
        'rw_k_a': gain((DEPTH, RW_WIDTH)),
        'rw_r_k': nrm((DEPTH, RW_HEADS, RW_HEAD_DIM), 0.1),
        'rw_gn_g': gain((DEPTH, RW_WIDTH)),
        'rw_gn_b': nrm((DEPTH, RW_WIDTH), 0.01),
        'mla_q_norm_g': gain((DEPTH, MLA_Q_RANK)),
        'mla_w_uq': nrm((DEPTH, MLA_Q_RANK, MLA_HEADS * MLA_QK), MLA_Q_RANK ** -0.5),
        'mla_kv_norm_g': gain((DEPTH, MLA_KV_RANK)),
        'mla_w_ukv': nrm((DEPTH, MLA_KV_RANK, MLA_HEADS * (MLA_NOPE + MLA_V)), MLA_KV_RANK ** -0.5),
        'mla_qn_g': gain((DEPTH, MLA_QK)),
        'mla_kn_g': gain((DEPTH, MLA_QK)),
        'df_qn_g': gain((DEPTH, DF_QK)),
        'df_kn_g': gain((DEPTH, DF_QK)),
        'df_lq1': nrm((DEPTH, DF_QK), 0.1),
        'df_lk1': nrm((DEPTH, DF_QK), 0.1),
        'df_lq2': nrm((DEPTH, DF_QK), 0.1),
        'df_lk2': nrm((DEPTH, DF_QK), 0.1),
        'df_subln_g': gain((DEPTH, DF_V)),
    }


def reference(x_prompt, x_sample, cache_mla_latent, cache_mla_krope, cache_diff_k, cache_diff_v,
              state_rwkv, state_rwkv_shift, page_table, ln_g, w_in, w_out, rw_mu, rw_w0, rw_w_up,
              rw_a0, rw_a_up, rw_k_k, rw_k_a, rw_r_k, rw_gn_g, rw_gn_b, mla_q_norm_g, mla_w_uq,
              mla_kv_norm_g, mla_w_ukv, mla_qn_g, mla_kn_g, df_qn_g, df_kn_g, df_lq1, df_lk1,
              df_lq2, df_lk2, df_subln_g):
    P = dict(ln_g=ln_g, w_in=w_in, w_out=w_out, rw_mu=rw_mu, rw_w0=rw_w0, rw_w_up=rw_w_up,
             rw_a0=rw_a0, rw_a_up=rw_a_up, rw_k_k=rw_k_k, rw_k_a=rw_k_a, rw_r_k=rw_r_k,
             rw_gn_g=rw_gn_g, rw_gn_b=rw_gn_b, mla_q_norm_g=mla_q_norm_g, mla_w_uq=mla_w_uq,
             mla_kv_norm_g=mla_kv_norm_g, mla_w_ukv=mla_w_ukv, mla_qn_g=mla_qn_g,
             mla_kn_g=mla_kn_g, df_qn_g=df_qn_g, df_kn_g=df_kn_g, df_lq1=df_lq1, df_lk1=df_lk1,
             df_lq2=df_lq2, df_lk2=df_lk2, df_subln_g=df_subln_g)

    pos_p = jnp.arange(x_prompt.shape[1])
    y_prompt, sp = trunk(x_prompt, pos_p, P, None)

    past = dict(mla_latent=cache_mla_latent, mla_krope=cache_mla_krope, diff_k=cache_diff_k,
                diff_v=cache_diff_v, rwkv_state=state_rwkv, rwkv_shift=state_rwkv_shift,
                page_table=page_table)
    n_past = page_table.shape[1] * PAGE_SIZE
    pos_s = n_past + jnp.arange(x_sample.shape[1])
    y_sample, ss = trunk(x_sample, pos_s, P, past)

    return (y_prompt, y_sample,
            sp['mla_latent'], sp['mla_krope'], sp['diff_k'], sp['diff_v'],
            sp['rwkv_state'], sp['rwkv_shift'],
            ss['mla_latent'], ss['mla_krope'], ss['diff_k'], ss['diff_v'],
            ss['rwkv_state'], ss['rwkv_shift'])
```

```python
import functools
import math

import jax
import jax.numpy as jnp
from jax import lax
from jax.experimental import pallas as pl
from jax.experimental.pallas import tpu as pltpu

F32 = jnp.float32
BF16 = jnp.bfloat16

D_MODEL = 2048
PAGE_SIZE = 128
RW_HEADS = 16
RW_HEAD_DIM = 64
RW_WIDTH = RW_HEADS * RW_HEAD_DIM
RW_LORA = 64
RW_SHIFT_DIM = 3 * RW_WIDTH + 2 * RW_LORA
MLA_HEADS = 4
MLA_NOPE = 128
MLA_ROPE = 64
MLA_QK = MLA_NOPE + MLA_ROPE
MLA_V = 128
MLA_WIDTH = MLA_HEADS * MLA_V
MLA_Q_RANK = 384
MLA_KV_RANK = 256
MLA_IN = MLA_Q_RANK + MLA_KV_RANK + MLA_ROPE
MLA_QPAD = 256
ROPE_THETA = 10000.0
MLA_SCALE = MLA_QK ** -0.5
DF_HEADS = 4
DF_QK = 64
DF_V = 128
DF_WIDTH = DF_HEADS * DF_V
DF_NQ = DF_HEADS * 2 * DF_QK
DF_KDIM = 2 * DF_QK
DF_IN = DF_NQ + DF_KDIM + DF_V
DF_SCALE = DF_QK ** -0.5
D_MIX = RW_WIDTH + MLA_WIDTH + DF_WIDTH
OFF_RW_G = RW_SHIFT_DIM
OFF_MLA = OFF_RW_G + RW_WIDTH
OFF_MLA_G = OFF_MLA + MLA_IN
OFF_DF = OFF_MLA_G + MLA_WIDTH
OFF_DF_G = OFF_DF + DF_IN
N_IN = OFF_DF_G + DF_WIDTH
NORM_EPS = 1e-6
GN_EPS = 64e-5
NEG_BIG = -1e30

LANES = 128
VMEM_LIMIT = 56 * 1024 * 1024


def _cparams(sem):
    return pltpu.CompilerParams(dimension_semantics=sem, vmem_limit_bytes=VMEM_LIMIT)


def _nt(a, b):
    return lax.dot_general(a, b, (((1,), (1,)), ((), ())), preferred_element_type=F32)


def _split_dot(x, g):
    hi = x.astype(BF16)
    lo = (x - hi.astype(F32)).astype(BF16)
    return (jnp.dot(hi, g, preferred_element_type=F32) + jnp.dot(lo, g, preferred_element_type=F32))


def _group_mats(width, group, ncol):
    r = lax.broadcasted_iota(jnp.int32, (width, ncol), 0) // group
    c = lax.broadcasted_iota(jnp.int32, (width, ncol), 1)
    red = jnp.where(r == c, 1.0, 0.0).astype(BF16)
    r2 = lax.broadcasted_iota(jnp.int32, (ncol, width), 0)
    c2 = lax.broadcasted_iota(jnp.int32, (ncol, width), 1) // group
    exp = jnp.where(r2 == c2, 1.0, 0.0).astype(BF16)
    return red, exp


def _group_sum(x, red, exp):
    return _split_dot(_split_dot(x, red), exp)


def _norm_kernel(x_ref, g_ref, h_ref):
    x = x_ref[...]
    ms = jnp.mean(x * x, axis=-1, keepdims=True)
    h_ref[...] = (x * lax.rsqrt(ms + NORM_EPS) * g_ref[...]).astype(BF16)


def _norm_rows(x, g, tm):
    rows, d = x.shape
    return pl.pallas_call(
        _norm_kernel,
        grid=(rows // tm,),
        in_specs=[pl.BlockSpec((tm, d), lambda i: (i, 0)), pl.BlockSpec((1, d), lambda i: (0, 0))],
        out_specs=pl.BlockSpec((tm, d), lambda i: (i, 0)),
        out_shape=jax.ShapeDtypeStruct((rows, d), BF16),
        compiler_params=_cparams(("parallel",)),
    )(x, g)


def _mm_kernel(h_ref, w_ref, o_ref):
    o_ref[...] = jnp.dot(h_ref[...], w_ref[...], preferred_element_type=F32)


def _matmul(h, w, tm, tn):
    rows, k = h.shape
    n = w.shape[1]
    return pl.pallas_call(
        _mm_kernel,
        grid=(rows // tm, n // tn),
        in_specs=[pl.BlockSpec((tm, k), lambda i, j: (i, 0)), pl.BlockSpec((k, tn), lambda i, j: (0, j))],
        out_specs=pl.BlockSpec((tm, tn), lambda i, j: (i, j)),
        out_shape=jax.ShapeDtypeStruct((rows, n), F32),
        compiler_params=_cparams(("parallel", "arbitrary")),
    )(h, w)


def _rw_prep_kernel(z_ref, prev_ref, mu_ref, w0_ref, wup_ref, a0_ref, aup_ref, kk_ref, ka_ref, rk_ref,
                    r_o, w_o, k_o, v_o, kk_o, kka_o, bonus_o):
    z = z_ref[...]
    zs = z + (prev_ref[...] - z) * mu_ref[...]
    r = zs[:, 0:RW_WIDTH]
    k = zs[:, RW_WIDTH:2 * RW_WIDTH]
    v = zs[:, 2 * RW_WIDTH:3 * RW_WIDTH]
    lora = zs[:, 3 * RW_WIDTH:RW_SHIFT_DIM]
    lane = lax.broadcasted_iota(jnp.int32, lora.shape, 1)
    lora = jnp.where(lane < RW_LORA, jnp.tanh(lora), lora).astype(BF16)
    wpre = w0_ref[...] + jnp.dot(lora, wup_ref[...], preferred_element_type=F32)
    apre = a0_ref[...] + jnp.dot(lora, aup_ref[...], preferred_element_type=F32)
    nw = -wpre
    softplus = jnp.maximum(nw, 0.0) + jnp.log1p(jnp.exp(-jnp.abs(nw)))
    decay = jnp.exp(-jnp.exp(-softplus - 0.5))
    a = jax.nn.sigmoid(apre)
    red, exp = _group_mats(RW_WIDTH, RW_HEAD_DIM, LANES)
    kk = k * kk_ref[...]
    nrm = jnp.maximum(jnp.sqrt(_group_sum(kk * kk, red, exp)), 1e-12)
    kk = kk / nrm
    kmod = k * (1.0 + (a - 1.0) * ka_ref[...])
    bonus = _group_sum(r * kmod * rk_ref[...], red, exp) * v
    r_o[...] = r
    w_o[...] = decay
    k_o[...] = kmod
    v_o[...] = v
    kk_o[...] = kk
    kka_o[...] = kk * a
    bonus_o[...] = bonus


def _rw_prep(z, prev, mu, w0, wup, a0, aup, k_k, k_a, r_k, tm):
    rows = z.shape[0]
    row_in = pl.BlockSpec((tm, RW_SHIFT_DIM), lambda i: (i, 0))
    vec = lambda n: pl.BlockSpec((1, n), lambda i: (0, 0))
    mat = pl.BlockSpec((2 * RW_LORA, RW_WIDTH), lambda i: (0, 0))
    row_out = pl.BlockSpec((tm, RW_WIDTH), lambda i: (i, 0))
    out = jax.ShapeDtypeStruct((rows, RW_WIDTH), F32)
    return pl.pallas_call(
        _rw_prep_kernel,
        grid=(rows // tm,),
        in_specs=[row_in, row_in, vec(RW_SHIFT_DIM), vec(RW_WIDTH), mat, vec(RW_WIDTH), mat,
                  vec(RW_WIDTH), vec(RW_WIDTH), vec(RW_WIDTH)],
        out_specs=[row_out] * 7,
        out_shape=[out] * 7,
        compiler_params=_cparams(("parallel",)),
    )(z, prev, mu, w0, wup, a0, aup, k_k, k_a, r_k)


SCAN_COLS = 128


def _rw_step(st, cols, v_row):
    kkc, wc, kkac, kc, rc = cols
    sk = jnp.sum(st * kkc, axis=0, keepdims=True)
    st = st * wc - kkac * sk + kc * v_row
    o = jnp.sum(st * rc, axis=0, keepdims=True)
    return st, o


def _col(xt, c):
    return jnp.broadcast_to(xt[:, c:c + 1], (RW_HEAD_DIM, RW_HEAD_DIM))


def _rw_scan_prompt_kernel(kk_ref, w_ref, kka_ref, k_ref, r_ref, v_ref, o_ref, sfin_ref, st_scr, o_scr):
    tb = pl.program_id(2)

    @pl.when(tb == 0)
    def _():
        st_scr[...] = jnp.zeros_like(st_scr)

    for g in range(2):
        xts = [ref[g] for ref in (kk_ref, w_ref, kka_ref, k_ref, r_ref)]
        vg = v_ref[:, g * RW_HEAD_DIM:(g + 1) * RW_HEAD_DIM]
        st = st_scr[g]
        for c in range(SCAN_COLS):
            st, o = _rw_step(st, [_col(xt, c) for xt in xts], vg[c:c + 1, :])
            o_scr[g, c:c + 1, :] = o
        st_scr[g] = st
    o_ref[...] = jnp.concatenate([o_scr[0], o_scr[1]], axis=1)

    @pl.when(tb == pl.num_programs(2) - 1)
    def _():
        sfin_ref[...] = st_scr[...]


def _rw_scan_prompt(xts, v, nb, t):
    xspec = pl.BlockSpec((None, 2, RW_HEAD_DIM, SCAN_COLS), lambda b, p, tb: (b, p, 0, tb))
    vspec = pl.BlockSpec((None, SCAN_COLS, 2 * RW_HEAD_DIM), lambda b, p, tb: (b, tb, p))
    sspec = pl.BlockSpec((None, 2, RW_HEAD_DIM, RW_HEAD_DIM), lambda b, p, tb: (b, p, 0, 0))
    return pl.pallas_call(
        _rw_scan_prompt_kernel,
        grid=(nb, RW_HEADS // 2, t // SCAN_COLS),
        in_specs=[xspec] * 5 + [vspec],
        out_specs=[vspec, sspec],
        out_shape=[jax.ShapeDtypeStruct((nb, t, RW_WIDTH), F32),
                   jax.ShapeDtypeStruct((nb, RW_HEADS, RW_HEAD_DIM, RW_HEAD_DIM), F32)],
        scratch_shapes=[pltpu.VMEM((2, RW_HEAD_DIM, RW_HEAD_DIM), F32),
                        pltpu.VMEM((2, SCAN_COLS, RW_HEAD_DIM), F32)],
        compiler_params=_cparams(("parallel", "parallel", "arbitrary")),
    )(*xts, v)


def _rw_scan_sample_kernel(kk_ref, w_ref, kka_ref, k_ref, r_ref, v_ref, s0_ref, o_ref, sfin_ref, o_scr,
                           *, seg):
    for g in range(2):
        xts = [ref[g] for ref in (kk_ref, w_ref, kka_ref, k_ref, r_ref)]
        vg = v_ref[:, g * RW_HEAD_DIM:(g + 1) * RW_HEAD_DIM]
        for bi in range(SCAN_COLS // seg):
            st = s0_ref[bi, g]
            for s in range(seg):
                c = bi * seg + s
                st, o = _rw_step(st, [_col(xt, c) for xt in xts], vg[c:c + 1, :])
                o_scr[g, c:c + 1, :] = o
            sfin_ref[bi, g] = st
    o_ref[...] = jnp.concatenate([o_scr[0], o_scr[1]], axis=1)


def _rw_scan_sample(xts, v, s0t, seg):
    rows = v.shape[0]
    nseg = SCAN_COLS // seg
    xspec = pl.BlockSpec((2, RW_HEAD_DIM, SCAN_COLS), lambda p, cb: (p, 0, cb))
    vspec = pl.BlockSpec((SCAN_COLS, 2 * RW_HEAD_DIM), lambda p, cb: (cb, p))
    sspec = pl.BlockSpec((nseg, 2, RW_HEAD_DIM, RW_HEAD_DIM), lambda p, cb: (cb, p, 0, 0))
    return pl.pallas_call(
        functools.partial(_rw_scan_sample_kernel, seg=seg),
        grid=(RW_HEADS // 2, rows // SCAN_COLS),
        in_specs=[xspec] * 5 + [vspec, sspec],
        out_specs=[vspec, sspec],
        out_shape=[jax.ShapeDtypeStruct((rows, RW_WIDTH), F32),
                   jax.ShapeDtypeStruct(s0t.shape, F32)],
        scratch_shapes=[pltpu.VMEM((2, SCAN_COLS, RW_HEAD_DIM), F32)],
        compiler_params=_cparams(("parallel", "parallel")),
    )(*xts, v, s0t)


def _mla_q_c_kr(z_ref, cs_ref, sn_ref, ckr_ref, qng_ref, wuq_ref, wuqs_ref, kvng_ref, qg_ref):
    z = z_ref[...]
    cq = z[:, 0:MLA_Q_RANK]
    cq = (cq * lax.rsqrt(jnp.mean(cq * cq, axis=-1, keepdims=True) + NORM_EPS) * qng_ref[...]).astype(BF16)
    q_pre = jnp.dot(cq, wuq_ref[...], preferred_element_type=F32)
    q_swp = jnp.dot(cq, wuqs_ref[...], preferred_element_type=F32)
    cs, sn, qg = cs_ref[...], sn_ref[...], qg_ref[...]
    qs = []
    for h in range(MLA_HEADS):
        sl = slice(h * MLA_QPAD, (h + 1) * MLA_QPAD)
        q = q_pre[:, sl] * cs + q_swp[:, sl] * sn
        ms = jnp.sum(q * q, axis=-1, keepdims=True) * (1.0 / MLA_QK)
        qs.append(q * lax.rsqrt(ms + NORM_EPS) * qg)
    c = z[:, MLA_Q_RANK:MLA_Q_RANK + MLA_KV_RANK]
    c = c * lax.rsqrt(jnp.mean(c * c, axis=-1, keepdims=True) + NORM_EPS) * kvng_ref[...]
    kr2 = z[:, MLA_Q_RANK + MLA_KV_RANK:] * ckr_ref[...]
    kr2 = kr2 + pltpu.roll(kr2, MLA_ROPE, axis=1)
    return qs, c, kr2


def _mla_prep_prompt_kernel(z_ref, cs_ref, sn_ref, ckr_ref, qng_ref, wuq_ref, wuqs_ref, kvng_ref, qg_ref,
                            wuk_ref, wuv_ref, kg_ref, q_o, c_o, kr_o, k_o, v_o):
    qs, c, kr2 = _mla_q_c_kr(z_ref, cs_ref, sn_ref, ckr_ref, qng_ref, wuq_ref, wuqs_ref, kvng_ref, qg_ref)
    for h in range(MLA_HEADS):
        q_o[:, h * MLA_QPAD:(h + 1) * MLA_QPAD] = (qs[h] * MLA_SCALE).astype(BF16)
    c_o[...] = c
    kr_o[...] = kr2[:, 0:MLA_ROPE]
    cb = c.astype(BF16)
    kn = jnp.dot(cb, wuk_ref[...], preferred_element_type=F32)
    v_o[...] = jnp.dot(cb, wuv_ref[...], preferred_element_type=F32).astype(BF16)
    lane = lax.broadcasted_iota(jnp.int32, kr2.shape, 1)
    krz = jnp.where(lane < MLA_ROPE, kr2, 0.0)
    kr_ss = jnp.sum(krz * krz, axis=-1, keepdims=True)
    kg = kg_ref[...]
    for h in range(MLA_HEADS):
        knh = kn[:, h * MLA_NOPE:(h + 1) * MLA_NOPE]
        ms = (jnp.sum(knh * knh, axis=-1, keepdims=True) + kr_ss) * (1.0 / MLA_QK)
        rs = lax.rsqrt(ms + NORM_EPS)
        k_o[:, h * MLA_QPAD:h * MLA_QPAD + MLA_NOPE] = (knh * rs * kg[:, 0:MLA_NOPE]).astype(BF16)
        k_o[:, h * MLA_QPAD + MLA_NOPE:(h + 1) * MLA_QPAD] = (krz * rs * kg[:, MLA_NOPE:]).astype(BF16)


def _mla_prep_sample_kernel(z_ref, cs_ref, sn_ref, ckr_ref, qng_ref, wuq_ref, wuqs_ref, kvng_ref, qg_ref,
                            wukt_ref, kg_ref, qa_o, qr_o, c_o, kr_o):
    qs, c, kr2 = _mla_q_c_kr(z_ref, cs_ref, sn_ref, ckr_ref, qng_ref, wuq_ref, wuqs_ref, kvng_ref, qg_ref)
    c_o[...] = c
    kr_o[...] = kr2[:, 0:MLA_ROPE]
    kg = kg_ref[...]
    for h in range(MLA_HEADS):
        qk = qs[h] * kg * MLA_SCALE
        qn = qk[:, 0:MLA_NOPE].astype(BF16)
        qa_o[:, h * MLA_KV_RANK:(h + 1) * MLA_KV_RANK] = jnp.dot(
            qn, wukt_ref[h * MLA_NOPE:(h + 1) * MLA_NOPE, :], preferred_element_type=F32).astype(BF16)
        qr_o[:, h * LANES:(h + 1) * LANES] = qk[:, MLA_NOPE:].astype(BF16)


def _mla_prep(z, tabs, lw, tm, prompt):
    rows = z.shape[0]
    cs, sn, ckr = tabs
    ntab = cs.shape[0] // tm
    row = lambda n: pl.BlockSpec((tm, n), lambda i: (i, 0))
    tab = lambda n: pl.BlockSpec((tm, n), lambda i: (i % ntab, 0))
    full = lambda a: pl.BlockSpec(a.shape, lambda i: (0,) * a.ndim)
    common = [z, cs, sn, ckr, lw['mla_q_norm_g'], lw['wuq'], lw['wuqs'], lw['mla_kv_norm_g'], lw['mla_qg']]
    common_specs = [row(z.shape[1]), tab(MLA_QPAD), tab(MLA_QPAD), tab(LANES)] + [full(a) for a in common[4:]]
    if prompt:
        extra = [lw['wuk'], lw['wuv'], lw['mla_kg']]
        outs = [(MLA_HEADS * MLA_QPAD, BF16), (MLA_KV_RANK, F32), (MLA_ROPE, F32),
                (MLA_HEADS * MLA_QPAD, BF16), (MLA_WIDTH, BF16)]
        body = _mla_prep_prompt_kernel
    else:
        extra = [lw['wukt'], lw['mla_kg']]
        outs = [(MLA_HEADS * MLA_KV_RANK, BF16), (MLA_HEADS * LANES, BF16), (MLA_KV_RANK, F32),
                (MLA_ROPE, F32)]
        body = _mla_prep_sample_kernel
    return pl.pallas_call(
        body,
        grid=(rows // tm,),
        in_specs=common_specs + [full(a) for a in extra],
        out_specs=[row(n) for n, _ in outs],
        out_shape=[jax.ShapeDtypeStruct((rows, n), dt) for n, dt in outs],
        compiler_params=_cparams(("parallel",)),
    )(*common, *extra)


def _df_prep_kernel(z_ref, qg_ref, kg_ref, q_o, k_o, v_o, kb_o, vb_o):
    z = z_ref[...]
    q = z[:, 0:DF_NQ]
    red, exp = _group_mats(DF_NQ, DF_QK, LANES)
    ms = _group_sum(q * q, red, exp) * (1.0 / DF_QK)
    q_o[...] = (q * lax.rsqrt(ms + NORM_EPS) * qg_ref[...] * DF_SCALE).astype(BF16)
    k = z[:, DF_NQ:DF_NQ + DF_KDIM]
    red, exp = _group_mats(DF_KDIM, DF_QK, LANES)
    ms = _group_sum(k * k, red, exp) * (1.0 / DF_QK)
    k = k * lax.rsqrt(ms + NORM_EPS) * kg_ref[...]
    v = z[:, DF_NQ + DF_KDIM:]
    k_o[...] = k
    v_o[...] = v
    kb_o[...] = k.astype(BF16)
    vb_o[...] = v.astype(BF16)


def _df_prep(z, qg, kg, tm):
    rows = z.shape[0]
    row = lambda n: pl.BlockSpec((tm, n), lambda i: (i, 0))
    full = lambda a: pl.BlockSpec(a.shape, lambda i: (0,) * a.ndim)
    outs = [(DF_NQ, BF16), (DF_KDIM, F32), (DF_V, F32), (DF_KDIM, BF16), (DF_V, BF16)]
    return pl.pallas_call(
        _df_prep_kernel,
        grid=(rows // tm,),
        in_specs=[row(DF_IN), full(qg), full(kg)],
        out_specs=[row(n) for n, _ in outs],
        out_shape=[jax.ShapeDtypeStruct((rows, n), dt) for n, dt in outs],
        compiler_params=_cparams(("parallel",)),
    )(z, qg, kg)


def _online_update(s, v, m_ref, l_ref, acc_ref, idx):
    m_old = m_ref[idx]
    m_new = jnp.maximum(m_old, jnp.max(s, axis=-1, keepdims=True))
    alpha = jnp.exp(m_old - m_new)
    p = jnp.exp(s - m_new)
    l_ref[idx] = alpha * l_ref[idx] + jnp.sum(p, axis=-1, keepdims=True)
    acc_ref[idx] = alpha * acc_ref[idx] + jnp.dot(p.astype(BF16), v, preferred_element_type=F32)
    m_ref[idx] = m_new


def _causal_mask(i, j, tq, tk):
    row = i * tq + lax.broadcasted_iota(jnp.int32, (tq, tk), 0)
    col = j * tk + lax.broadcasted_iota(jnp.int32, (tq, tk), 1)
    return col <= row


def _flash_init(m_ref, l_ref, acc_ref):
    m_ref[...] = jnp.full(m_ref.shape, NEG_BIG, F32)
    l_ref[...] = jnp.zeros_like(l_ref)
    acc_ref[...] = jnp.zeros_like(acc_ref)


def _mla_flash_kernel(q_ref, k_ref, v_ref, o_ref, m_ref, l_ref, acc_ref, *, tq, tk):
    i, j = pl.program_id(1), pl.program_id(2)

    @pl.when(j == 0)
    def _():
        _flash_init(m_ref, l_ref, acc_ref)

    @pl.when(j <= i)
    def _():
        mask = _causal_mask(i, j, tq, tk)
        for h in range(MLA_HEADS):
            sl = slice(h * MLA_QPAD, (h + 1) * MLA_QPAD)
            s = jnp.where(mask, _nt(q_ref[:, sl], k_ref[:, sl]), NEG_BIG)
            _online_update(s, v_ref[:, h * MLA_V:(h + 1) * MLA_V], m_ref, l_ref, acc_ref, h)

    @pl.when(j == i)
    def _():
        for h in range(MLA_HEADS):
            o_ref[:, h * MLA_V:(h + 1) * MLA_V] = acc_ref[h] / l_ref[h]


def _df_lambda(lq1, lk1, lq2, lk2, lam_init):
    return (jnp.exp(jnp.sum(lq1 * lk1, axis=-1, keepdims=True))
            - jnp.exp(jnp.sum(lq2 * lk2, axis=-1, keepdims=True)) + lam_init)


def _df_merge_norm(o1, o2, lam_refs, lamc_ref, sg_ref):
    lq1, lk1, lq2, lk2 = (r[...] for r in lam_refs)
    lam_init = lamc_ref[:, 0:1]
    lam = _df_lambda(lq1, lk1, lq2, lk2, lam_init)
    o = o1 - lam * o2
    o = o * lax.rsqrt(jnp.mean(o * o, axis=-1, keepdims=True) + NORM_EPS) * sg_ref[...]
    return o * (1.0 - lam_init)


def _df_flash_kernel(q_ref, k_ref, v_ref, lq1, lk1, lq2, lk2, lamc_ref, sg_ref, o_ref,
                     qm_ref, m_ref, l_ref, acc_ref, *, tq, tk):
    i, j = pl.program_id(1), pl.program_id(2)

    @pl.when(j == 0)
    def _():
        _flash_init(m_ref, l_ref, acc_ref)
        lane = lax.broadcasted_iota(jnp.int32, (tq, 2 * DF_QK), 1)
        for h in range(DF_HEADS):
            q = q_ref[:, h * 2 * DF_QK:(h + 1) * 2 * DF_QK]
            qm_ref[2 * h] = jnp.where(lane < DF_QK, q, jnp.zeros_like(q))
            qm_ref[2 * h + 1] = jnp.where(lane >= DF_QK, q, jnp.zeros_like(q))

    @pl.when(j <= i)
    def _():
        mask = _causal_mask(i, j, tq, tk)
        k, v = k_ref[...], v_ref[...]
        for vh in range(2 * DF_HEADS):
            s = jnp.where(mask, _nt(qm_ref[vh], k), NEG_BIG)
            _online_update(s, v, m_ref, l_ref, acc_ref, vh)

    @pl.when(j == i)
    def _():
        for h in range(DF_HEADS):
            o1 = acc_ref[2 * h] / l_ref[2 * h]
            o2 = acc_ref[2 * h + 1] / l_ref[2 * h + 1]
            o_ref[:, h * DF_V:(h + 1) * DF_V] = _df_merge_norm(o1, o2, (lq1, lk1, lq2, lk2), lamc_ref, sg_ref)


def _flash_specs(nq, tq, tk, wq, wk, wv, wo):
    qspec = pl.BlockSpec((None, tq, wq), lambda b, i, j: (b, i, 0))
    kspec = pl.BlockSpec((None, tk, wk), lambda b, i, j: (b, jnp.minimum(i, j), 0))
    vspec = pl.BlockSpec((None, tk, wv), lambda b, i, j: (b, jnp.minimum(i, j), 0))
    ospec = pl.BlockSpec((None, tq, wo), lambda b, i, j: (b, i, 0))
    return qspec, kspec, vspec, ospec


def _mla_flash(q, k, v, tq):
    nb, t, _ = q.shape
    nq = t // tq
    qspec, kspec, vspec, ospec = _flash_specs(nq, tq, tq, q.shape[2], k.shape[2], v.shape[2], MLA_WIDTH)
    return pl.pallas_call(
        functools.partial(_mla_flash_kernel, tq=tq, tk=tq),
        grid=(nb, nq, nq),
        in_specs=[qspec, kspec, vspec],
        out_specs=ospec,
        out_shape=jax.ShapeDtypeStruct((nb, t, MLA_WIDTH), F32),
        scratch_shapes=[pltpu.VMEM((MLA_HEADS, tq, 1), F32), pltpu.VMEM((MLA_HEADS, tq, 1), F32),
                        pltpu.VMEM((MLA_HEADS, tq, MLA_V), F32)],
        compiler_params=_cparams(("parallel", "parallel", "arbitrary")),
    )(q, k, v)


def _df_flash(q, k, v, lam_args, tq):
    nb, t, _ = q.shape
    nq = t // tq
    qspec, kspec, vspec, ospec = _flash_specs(nq, tq, tq, q.shape[2], k.shape[2], v.shape[2], DF_WIDTH)
    small = [pl.BlockSpec(a.shape, lambda b, i, j: (0, 0)) for a in lam_args]
    nvh = 2 * DF_HEADS
    return pl.pallas_call(
        functools.partial(_df_flash_kernel, tq=tq, tk=tq),
        grid=(nb, nq, nq),
        in_specs=[qspec, kspec, vspec] + small,
        out_specs=ospec,
        out_shape=jax.ShapeDtypeStruct((nb, t, DF_WIDTH), F32),
        scratch_shapes=[pltpu.VMEM((nvh, tq, 2 * DF_QK), BF16), pltpu.VMEM((nvh, tq, 1), F32),
                        pltpu.VMEM((nvh, tq, 1), F32), pltpu.VMEM((nvh, tq, DF_V), F32)],
        compiler_params=_cparams(("parallel", "parallel", "arbitrary")),
    )(q, k, v, *lam_args)


NQ_MLA = 16
NQ_DF = 32
NEW_PAD = 128


def _paged_kernel(pt_ref, layer_ref,
                  qa_ref, qr_ref, qd_ref, cn_ref, krn_ref, kdn_ref, vdn_ref,
                  wukt_ref, wuv_ref, lq1, lk1, lq2, lk2, lamc_ref, sg_ref,
                  cc_hbm, ckr_hbm, ck_hbm, cv_hbm,
                  omla_ref, odf_ref,
                  cbuf, krbuf, kbuf, vbuf, sems, lhs_ref, m1, l1, acc1, m2, l2, acc2,
                  *, pages, nchunk, nstep):
    b = pl.program_id(0)
    nb = pl.num_programs(0)
    layer = layer_ref[0]
    lc = pages * PAGE_SIZE

    def copies(bb, ci, slot):
        out = []
        for n, (hbm, buf) in enumerate(((cc_hbm, cbuf), (ckr_hbm, krbuf), (ck_hbm, kbuf), (cv_hbm, vbuf))):
            for p in range(pages):
                page = pt_ref[bb, ci * pages + p]
                out.append(pltpu.make_async_copy(
                    hbm.at[layer, page], buf.at[slot, pl.ds(p * PAGE_SIZE, PAGE_SIZE)], sems.at[n, slot]))
        return out

    @pl.when(b == 0)
    def _():
        for cp in copies(0, 0, 0):
            cp.start()

    lhs_ref[0:MLA_HEADS * MLA_NOPE, :] = wukt_ref[...]
    lhs_ref[MLA_HEADS * MLA_NOPE:, :] = qa_ref[...]
    m1[...] = jnp.full(m1.shape, NEG_BIG, F32)
    l1[...] = jnp.zeros_like(l1)
    acc1[...] = jnp.zeros_like(acc1)
    m2[...] = jnp.full(m2.shape, NEG_BIG, F32)
    l2[...] = jnp.zeros_like(l2)
    acc2[...] = jnp.zeros_like(acc2)
    ones8 = jnp.ones((8, MLA_ROPE), BF16)

    def update(s, vals, m_ref, l_ref, acc_ref):
        m_old = m_ref[...]
        m_new = jnp.maximum(m_old, jnp.max(s, axis=-1, keepdims=True))
        alpha = jnp.exp(m_old - m_new)
        p = jnp.exp(s - m_new)
        l_ref[...] = alpha * l_ref[...] + jnp.sum(p, axis=-1, keepdims=True)
        acc_ref[...] = alpha * acc_ref[...] + jnp.dot(p.astype(BF16), vals, preferred_element_type=F32)
        m_ref[...] = m_new

    def attend(c, kr, kd, vd, mask):
        n = c.shape[0]
        big = _nt(lhs_ref[...], c)
        kn = big[0:MLA_HEADS * MLA_NOPE]
        ss_n = jnp.sum((kn * kn).reshape(MLA_HEADS, MLA_NOPE, n), axis=1)
        ss_r = _nt(ones8, (kr * kr).astype(BF16))[0:1]
        rs = lax.rsqrt((ss_n + ss_r) * (1.0 / MLA_QK) + NORM_EPS)
        s = big[MLA_HEADS * MLA_NOPE:] + _nt(qr_ref[...], kr.astype(BF16))
        head = lax.broadcasted_iota(jnp.int32, (NQ_MLA, n), 0) // nstep
        rs_rows = jnp.zeros((NQ_MLA, n), F32)
        for h in range(MLA_HEADS):
            rs_rows = jnp.where(head == h, rs[h:h + 1, :], rs_rows)
        s = s * rs_rows
        sd = _nt(qd_ref[...], kd)
        if mask is not None:
            s = jnp.where(mask[0], s, NEG_BIG)
            sd = jnp.where(mask[1], sd, NEG_BIG)
        update(s, c, m1, l1, acc1)
        update(sd, vd, m2, l2, acc2)

    def chunk(ci, carry):
        slot = ci % 2
        last = ci == nchunk - 1
        nbb = jnp.where(last, b + 1, b)
        nci = jnp.where(last, 0, ci + 1)

        @pl.when(nbb < nb)
        def _():
            for cp in copies(nbb, nci, 1 - slot):
                cp.start()

        for cp in copies(b, ci, slot):
            cp.wait()
        attend(cbuf[slot].astype(BF16), krbuf[slot], kbuf[slot].astype(BF16), vbuf[slot].astype(BF16), None)
        return carry

    lax.fori_loop(0, nchunk, chunk, 0)

    key = lax.broadcasted_iota(jnp.int32, (NQ_MLA, NEW_PAD), 1)
    step = lax.broadcasted_iota(jnp.int32, (NQ_MLA, NEW_PAD), 0) % nstep
    keyd = lax.broadcasted_iota(jnp.int32, (NQ_DF, NEW_PAD), 1)
    stepd = lax.broadcasted_iota(jnp.int32, (NQ_DF, NEW_PAD), 0) % nstep
    attend(cn_ref[...], krn_ref[...], kdn_ref[...], vdn_ref[...], (key <= step, keyd <= stepd))

    o_abs = (acc1[...] / l1[...]).astype(BF16)
    full = jnp.dot(o_abs, wuv_ref[...], preferred_element_type=F32)
    head = lax.broadcasted_iota(jnp.int32, (NQ_MLA, MLA_V), 0) // nstep
    o = jnp.zeros((NQ_MLA, MLA_V), F32)
    for h in range(MLA_HEADS):
        o = jnp.where(head == h, full[:, h * MLA_V:(h + 1) * MLA_V], o)
    omla_ref[...] = o
    od = acc2[...] / l2[...]
    odf_ref[...] = _df_merge_norm(od[0:NQ_DF // 2], od[NQ_DF // 2:], (lq1, lk1, lq2, lk2), lamc_ref, sg_ref)


def _paged_attention(page_table, layer, qa, qr, qd, cn, krn, kdn, vdn, lw, caches, pages, nstep):
    nb, npages = page_table.shape
    nchunk = npages // pages
    lc = pages * PAGE_SIZE
    per_b = lambda a: pl.BlockSpec((None,) + a.shape[1:], lambda b, pt, ly: (b,) + (0,) * (a.ndim - 1))
    full = lambda a: pl.BlockSpec(a.shape, lambda b, pt, ly: (0,) * a.ndim)
    any_spec = pl.BlockSpec(memory_space=pl.ANY)
    batched = [qa, qr, qd, cn, krn, kdn, vdn]
    shared = [lw['wukt'], lw['wuv'], lw['df_lq1'], lw['df_lk1'], lw['df_lq2'], lw['df_lk2'], lw['lamc'],
              lw['df_subln_g']]
    ospec = pl.BlockSpec((None, NQ_MLA, LANES), lambda b, pt, ly: (b, 0, 0))
    grid_spec = pltpu.PrefetchScalarGridSpec(
        num_scalar_prefetch=2,
        grid=(nb,),
        in_specs=[per_b(a) for a in batched] + [full(a) for a in shared] + [any_spec] * 4,
        out_specs=[ospec, ospec],
        scratch_shapes=[pltpu.VMEM((2, lc, MLA_KV_RANK), F32), pltpu.VMEM((2, lc, MLA_ROPE), F32),
                        pltpu.VMEM((2, lc, DF_KDIM), F32), pltpu.VMEM((2, lc, DF_V), F32),
                        pltpu.SemaphoreType.DMA((4, 2)),
                        pltpu.VMEM((MLA_HEADS * MLA_NOPE + NQ_MLA, MLA_KV_RANK), BF16),
                        pltpu.VMEM((NQ_MLA, 1), F32), pltpu.VMEM((NQ_MLA, 1), F32),
                        pltpu.VMEM((NQ_MLA, MLA_KV_RANK), F32),
                        pltpu.VMEM((NQ_DF, 1), F32), pltpu.VMEM((NQ_DF, 1), F32),
                        pltpu.VMEM((NQ_DF, DF_V), F32)])
    return pl.pallas_call(
        functools.partial(_paged_kernel, pages=pages, nchunk=nchunk, nstep=nstep),
        grid_spec=grid_spec,
        out_shape=[jax.ShapeDtypeStruct((nb, NQ_MLA, MLA_V), F32), jax.ShapeDtypeStruct((nb, NQ_MLA, DF_V), F32)],
        compiler_params=_cparams(("arbitrary",)),
    )(page_table, layer, *batched, *shared, *caches)


def _out_kernel(x_ref, orw_ref, bonus_ref, omla_ref, odf_ref, g_ref, wout_ref, gng_ref, gnb_ref, lng_ref,
                xo_ref, h_ref):
    red, exp = _group_mats(RW_WIDTH, RW_HEAD_DIM, LANES)
    o = orw_ref[...]
    mean = _group_sum(o, red, exp) * (1.0 / RW_HEAD_DIM)
    d = o - mean
    var = _group_sum(d * d, red, exp) * (1.0 / RW_HEAD_DIM)
    y_rw = d * lax.rsqrt(var + GN_EPS) * gng_ref[...] + gnb_ref[...] + bonus_ref[...]
    g = g_ref[...]
    sg = g * jax.nn.sigmoid(g)
    a0, a1 = RW_WIDTH, RW_WIDTH + MLA_WIDTH
    acc = jnp.dot((y_rw * sg[:, 0:a0]).astype(BF16), wout_ref[0:a0, :], preferred_element_type=F32)
    acc += jnp.dot((omla_ref[...] * sg[:, a0:a1]).astype(BF16), wout_ref[a0:a1, :], preferred_element_type=F32)
    acc += jnp.dot((odf_ref[...] * sg[:, a1:]).astype(BF16), wout_ref[a1:, :], preferred_element_type=F32)
    xn = x_ref[...] + acc
    xo_ref[...] = xn
    ms = jnp.mean(xn * xn, axis=-1, keepdims=True)
    h_ref[...] = (xn * lax.rsqrt(ms + NORM_EPS) * lng_ref[...]).astype(BF16)


def _out_proj(x, o_rw, bonus, o_mla, o_df, gates, wout, gn_g, gn_b, ln_g_next, tm):
    rows = x.shape[0]
    row = lambda n: pl.BlockSpec((tm, n), lambda i: (i, 0))
    full = lambda a: pl.BlockSpec(a.shape, lambda i: (0,) * a.ndim)
    return pl.pallas_call(
        _out_kernel,
        grid=(rows // tm,),
        in_specs=[row(D_MODEL), row(RW_WIDTH), row(RW_WIDTH), row(MLA_WIDTH), row(DF_WIDTH), row(D_MIX),
                  full(wout), full(gn_g), full(gn_b), full(ln_g_next)],
        out_specs=[row(D_MODEL), row(D_MODEL)],
        out_shape=[jax.ShapeDtypeStruct((rows, D_MODEL), F32), jax.ShapeDtypeStruct((rows, D_MODEL), BF16)],
        compiler_params=_cparams(("parallel",)),
    )(x, o_rw, bonus, o_mla, o_df, gates, wout, gn_g, gn_b, ln_g_next)


def _prep_weights(P):
    depth = P['w_in'].shape[0]
    w_in = P['w_in']
    kr0 = OFF_MLA + MLA_Q_RANK + MLA_KV_RANK
    half = MLA_ROPE // 2
    kr_swapped = jnp.concatenate([w_in[:, :, kr0 + half:kr0 + MLA_ROPE], w_in[:, :, kr0:kr0 + half]], axis=-1)
    W = {
        'w_rw': w_in[:, :, 0:OFF_RW_G].astype(BF16),
        'w_gate': jnp.concatenate([w_in[:, :, OFF_RW_G:OFF_MLA], w_in[:, :, OFF_MLA_G:OFF_DF],
                                   w_in[:, :, OFF_DF_G:N_IN]], axis=-1).astype(BF16),
        'w_mla': jnp.concatenate([w_in[:, :, OFF_MLA:OFF_MLA_G], kr_swapped], axis=-1).astype(BF16),
        'w_df': w_in[:, :, OFF_DF:OFF_DF_G].astype(BF16),
        'w_out': P['w_out'].astype(BF16),
    }
    zl = jnp.zeros((depth, RW_LORA, RW_WIDTH), F32)
    W['wup'] = jnp.concatenate([P['rw_w_up'], zl], axis=1).astype(BF16)
    W['aup'] = jnp.concatenate([zl, P['rw_a_up']], axis=1).astype(BF16)
    uq = P['mla_w_uq'].reshape(depth, MLA_Q_RANK, MLA_HEADS, MLA_QK)
    nope, rope = uq[..., :MLA_NOPE], uq[..., MLA_NOPE:]
    rope_sw = jnp.concatenate([rope[..., half:], rope[..., :half]], axis=-1)
    z64 = jnp.zeros(rope.shape, F32)
    z128 = jnp.zeros(nope.shape, F32)
    W['wuq'] = jnp.concatenate([nope, rope, z64], axis=-1).reshape(depth, MLA_Q_RANK, -1).astype(BF16)
    W['wuqs'] = jnp.concatenate([z128, rope_sw, z64], axis=-1).reshape(depth, MLA_Q_RANK, -1).astype(BF16)
    ukv = P['mla_w_ukv'].reshape(depth, MLA_KV_RANK, MLA_HEADS, MLA_NOPE + MLA_V)
    wuk = ukv[..., :MLA_NOPE].reshape(depth, MLA_KV_RANK, -1)
    W['wuk'] = wuk.astype(BF16)
    W['wukt'] = jnp.swapaxes(wuk, 1, 2).astype(BF16)
    W['wuv'] = ukv[..., MLA_NOPE:].reshape(depth, MLA_KV_RANK, -1).astype(BF16)
    pad = jnp.zeros((depth, MLA_QPAD - MLA_QK), F32)
    W['mla_qg'] = jnp.concatenate([P['mla_qn_g'], pad], axis=-1)
    W['mla_kg'] = jnp.concatenate([P['mla_kn_g'], pad], axis=-1)
    W['df_qg'] = jnp.tile(P['df_qn_g'], (1, DF_NQ // DF_QK))
    W['df_kg'] = jnp.tile(P['df_kn_g'], (1, DF_KDIM // DF_QK))
    W['rw_r_k'] = P['rw_r_k'].reshape(depth, RW_WIDTH)
    lam_init = jnp.asarray([0.8 - 0.6 * math.exp(-0.3 * i) for i in range(depth)], F32)
    W['lamc'] = jnp.broadcast_to(lam_init[:, None], (depth, LANES))
    for name in ('ln_g', 'rw_mu', 'rw_w0', 'rw_a0', 'rw_k_k', 'rw_k_a', 'rw_gn_g', 'rw_gn_b', 'mla_q_norm_g',
                 'mla_kv_norm_g', 'df_lq1', 'df_lk1', 'df_lq2', 'df_lk2', 'df_subln_g'):
        W[name] = P[name]
    return W


def _layer_weights(W, i):
    out = {}
    for name, a in W.items():
        out[name] = a[i] if a.ndim == 3 else a[i][None, :]
    return out


def _rope_tables(pos):
    half = MLA_ROPE // 2
    inv = ROPE_THETA ** (-jnp.arange(half, dtype=F32) / half)
    ang = pos.astype(F32)[:, None] * inv[None, :]
    cos, sin = jnp.cos(ang), jnp.sin(ang)
    n = pos.shape[0]
    cc = jnp.concatenate([cos, cos], axis=-1)
    ss = jnp.concatenate([-sin, sin], axis=-1)
    cs = jnp.concatenate([jnp.ones((n, MLA_NOPE), F32), cc, jnp.zeros((n, MLA_QPAD - MLA_QK), F32)], axis=-1)
    sn = jnp.concatenate([jnp.zeros((n, MLA_NOPE), F32), ss, jnp.zeros((n, MLA_QPAD - MLA_QK), F32)], axis=-1)
    ckr = jnp.concatenate([cc, ss], axis=-1)
    return cs, sn, ckr


def _pick(n, prefs):
    for p in prefs:
        if n % p == 0:
            return p
    return n


def _trunk(x, W, past, depth):
    nb, t, _ = x.shape
    rows = nb * t
    prompt = past is None
    tm = _pick(rows, (1024, 512, 256, 128))
    tm_small = _pick(rows, (512, 256, 128))
    tm_out = _pick(rows, (256, 128))
    tm_rw = _pick(rows, (256, 128))
    if prompt:
        pos = jnp.arange(t)
        tabs = _rope_tables(pos)
        tq = _pick(t, (512, 256, 128))
    else:
        npages = past['page_table'].shape[1]
        pos = npages * PAGE_SIZE + jnp.arange(t)
        tabs = tuple(jnp.tile(a, (nb, 1)) for a in _rope_tables(pos))
        pages = _pick(npages, (16, 8, 4, 2, 1))
        if npages // pages < 2 or (npages // pages) % 2:
            pages = max(1, pages // 2)
    xf = x.reshape(rows, D_MODEL)
    h = _norm_rows(xf, W['ln_g'][0][None, :], tm_small)
    new = {k: [] for k in ('mla_latent', 'mla_krope', 'diff_k', 'diff_v', 'rwkv_state', 'rwkv_shift')}
    for i in range(depth):
        lw = _layer_weights(W, i)
        z_rw = _matmul(h, lw['w_rw'], tm, _pick(RW_SHIFT_DIM, (640,)))
        gates = _matmul(h, lw['w_gate'], tm, _pick(D_MIX, (1024,)))
        z_mla = _matmul(h, lw['w_mla'], tm, lw['w_mla'].shape[1])
        z_df = _matmul(h, lw['w_df'], tm, DF_IN)

        z3 = z_rw.reshape(nb, t, RW_SHIFT_DIM)
        shift0 = jnp.zeros((nb, RW_SHIFT_DIM), F32) if prompt else past['rwkv_shift'][i]
        prev = jnp.concatenate([shift0[:, None, :], z3[:, :-1]], axis=1).reshape(rows, RW_SHIFT_DIM)
        r, w, kmod, v, kk, kka, bonus = _rw_prep(
            z_rw, prev, lw['rw_mu'], lw['rw_w0'], lw['wup'], lw['rw_a0'], lw['aup'],
            lw['rw_k_k'], lw['rw_k_a'], lw['rw_r_k'], tm_rw)
        if prompt:
            xts = [a.reshape(nb, t, RW_HEADS, RW_HEAD_DIM).transpose(0, 2, 3, 1) for a in (kk, w, kka, kmod, r)]
            o_rw, st = _rw_scan_prompt(xts, v.reshape(nb, t, RW_WIDTH), nb, t)
            o_rw = o_rw.reshape(rows, RW_WIDTH)
        else:
            xts = [a.reshape(rows, RW_HEADS, RW_HEAD_DIM).transpose(1, 2, 0) for a in (kk, w, kka, kmod, r)]
            s0t = jnp.swapaxes(past['rwkv_state'][i], -1, -2)
            o_rw, st = _rw_scan_sample(xts, v, s0t, t)
        new['rwkv_state'].append(jnp.swapaxes(st, -1, -2))
        new['rwkv_shift'].append(z3[:, -1])

        qd, kd, vd, kdb, vdb = _df_prep(z_df, lw['df_qg'], lw['df_kg'], tm_small)
        lam_args = [lw['df_lq1'], lw['df_lk1'], lw['df_lq2'], lw['df_lk2'], lw['lamc'], lw['df_subln_g']]
        if prompt:
            q, c, kr, k, vm = _mla_prep(z_mla, tabs, lw, tm_small, True)
            o_mla = _mla_flash(q.reshape(nb, t, -1), k.reshape(nb, t, -1), vm.reshape(nb, t, -1), tq)
            o_df = _df_flash(qd.reshape(nb, t, -1), kdb.reshape(nb, t, -1), vdb.reshape(nb, t, -1), lam_args, tq)
            o_mla = o_mla.reshape(rows, MLA_WIDTH)
            o_df = o_df.reshape(rows, DF_WIDTH)
        else:
            qa, qr, c, kr = _mla_prep(z_mla, tabs, lw, tm_small, False)
            qa = qa.reshape(nb, t, MLA_HEADS, MLA_KV_RANK).transpose(0, 2, 1, 3).reshape(nb, -1, MLA_KV_RANK)
            qr = qr.reshape(nb, t, MLA_HEADS, LANES)[..., :MLA_ROPE].transpose(0, 2, 1, 3)
            qr = qr.reshape(nb, -1, MLA_ROPE)
            q5 = qd.reshape(nb, t, DF_HEADS, 2, DF_QK).transpose(0, 3, 2, 1, 4)
            zq = jnp.zeros_like(q5[:, 0])
            qdm = jnp.stack([jnp.concatenate([q5[:, 0], zq], axis=-1),
                             jnp.concatenate([zq, q5[:, 1]], axis=-1)], axis=1).reshape(nb, -1, 2 * DF_QK)
            padk = lambda a: jnp.pad(a.reshape(nb, t, -1), ((0, 0), (0, NEW_PAD - t), (0, 0)))
            layer = jnp.full((1,), i, jnp.int32)
            caches = (past['mla_latent'], past['mla_krope'], past['diff_k'], past['diff_v'])
            o_mla, o_df = _paged_attention(
                past['page_table'], layer, qa, qr, qdm, padk(c).astype(BF16), padk(kr), padk(kdb), padk(vdb),
                lw, caches, pages, t)
            o_mla = o_mla.reshape(nb, MLA_HEADS, t, MLA_V).transpose(0, 2, 1, 3).reshape(rows, MLA_WIDTH)
            o_df = o_df.reshape(nb, DF_HEADS, t, DF_V).transpose(0, 2, 1, 3).reshape(rows, DF_WIDTH)
        new['mla_latent'].append(c.reshape(nb, t, MLA_KV_RANK))
        new['mla_krope'].append(kr.reshape(nb, t, MLA_ROPE))
        new['diff_k'].append(kd.reshape(nb, t, DF_KDIM))
        new['diff_v'].append(vd.reshape(nb, t, DF_V))

        ln_next = W['ln_g'][min(i + 1, depth - 1)][None, :]
        xf, h = _out_proj(xf, o_rw, bonus, o_mla, o_df, gates, lw['w_out'], lw['rw_gn_g'], lw['rw_gn_b'],
                          ln_next, tm_out)
    return xf.reshape(nb, t, D_MODEL), {k: jnp.stack(vs) for k, vs in new.items()}


def kernel(x_prompt, x_sample, cache_mla_latent, cache_mla_krope, cache_diff_k, cache_diff_v, state_rwkv, state_rwkv_shift, page_table, ln_g, w_in, w_out, rw_mu, rw_w0, rw_w_up, rw_a0, rw_a_up, rw_k_k, rw_k_a, rw_r_k, rw_gn_g, rw_gn_b, mla_q_norm_g, mla_w_uq, mla_kv_norm_g, mla_w_ukv, mla_qn_g, mla_kn_g, df_qn_g, df_kn_g, df_lq1, df_lk1, df_lq2, df_lk2, df_subln_g):
    P = dict(ln_g=ln_g, w_in=w_in, w_out=w_out, rw_mu=rw_mu, rw_w0=rw_w0, rw_w_up=rw_w_up,
             rw_a0=rw_a0, rw_a_up=rw_a_up, rw_k_k=rw_k_k, rw_k_a=rw_k_a, rw_r_k=rw_r_k,
             rw_gn_g=rw_gn_g, rw_gn_b=rw_gn_b, mla_q_norm_g=mla_q_norm_g, mla_w_uq=mla_w_uq,
             mla_kv_norm_g=mla_kv_norm_g, mla_w_ukv=mla_w_ukv, mla_qn_g=mla_qn_g,
             mla_kn_g=mla_kn_g, df_qn_g=df_qn_g, df_kn_g=df_kn_g, df_lq1=df_lq1, df_lk1=df_lk1,
             df_lq2=df_lq2, df_lk2=df_lk2, df_subln_g=df_subln_g)
    depth = w_in.shape[0]
    W = _prep_weights(P)
    y_p, sp = _trunk(x_prompt, W, None, depth)
    past = dict(mla_latent=cache_mla_latent, mla_krope=cache_mla_krope, diff_k=cache_diff_k,
                diff_v=cache_diff_v, rwkv_state=state_rwkv, rwkv_shift=state_rwkv_shift,
                page_table=page_table)
    y_s, ss = _trunk(x_sample, W, past, depth)
    keys = ('mla_latent', 'mla_krope', 'diff_k', 'diff_v', 'rwkv_state', 'rwkv_shift')
    return (y_p, y_s) + tuple(sp[k] for k in keys) + tuple(ss[k] for k in keys)
```

```python
import functools
import math

import jax
import jax.numpy as jnp
from jax import lax
from jax.experimental import pallas as pl
from jax.experimental.pallas import tpu as pltpu

F32 = jnp.float32
BF16 = jnp.bfloat16

D_MODEL = 2048
PAGE_SIZE = 128
RW_HEADS = 16
RW_HEAD_DIM = 64
RW_WIDTH = RW_HEADS * RW_HEAD_DIM
RW_LORA = 64
RW_SHIFT_DIM = 3 * RW_WIDTH + 2 * RW_LORA
MLA_HEADS = 4
MLA_NOPE = 128
MLA_ROPE = 64
MLA_QK = MLA_NOPE + MLA_ROPE
MLA_V = 128
MLA_WIDTH = MLA_HEADS * MLA_V
MLA_Q_RANK = 384
MLA_KV_RANK = 256
MLA_IN = MLA_Q_RANK + MLA_KV_RANK + MLA_ROPE
MLA_QPAD = 256
ROPE_THETA = 10000.0
MLA_SCALE = MLA_QK ** -0.5
DF_HEADS = 4
DF_QK = 64
DF_V = 128
DF_WIDTH = DF_HEADS * DF_V
DF_NQ = DF_HEADS * 2 * DF_QK
DF_KDIM = 2 * DF_QK
DF_IN = DF_NQ + DF_KDIM + DF_V
DF_SCALE = DF_QK ** -0.5
D_MIX = RW_WIDTH + MLA_WIDTH + DF_WIDTH
OFF_RW_G = RW_SHIFT_DIM
OFF_MLA = OFF_RW_G + RW_WIDTH
OFF_MLA_G = OFF_MLA + MLA_IN
OFF_DF = OFF_MLA_G + MLA_WIDTH
OFF_DF_G = OFF_DF + DF_IN
N_IN = OFF_DF_G + DF_WIDTH
NORM_EPS = 1e-6
GN_EPS = 64e-5
NEG_BIG = -1e30

LANES = 128
VMEM_LIMIT = 56 * 1024 * 1024


def _cparams(sem):
    return pltpu.CompilerParams(dimension_semantics=sem, vmem_limit_bytes=VMEM_LIMIT)


def _nt(a, b):
    return lax.dot_general(a, b, (((1,), (1,)), ((), ())), preferred_element_type=F32)


def _split_dot(x, g):
    hi = x.astype(BF16)
    lo = (x - hi.astype(F32)).astype(BF16)
    return (jnp.dot(hi, g, preferred_element_type=F32) + jnp.dot(lo, g, preferred_element_type=F32))


def _group_mats(width, group, ncol):
    r = lax.broadcasted_iota(jnp.int32, (width, ncol), 0) // group
    c = lax.broadcasted_iota(jnp.int32, (width, ncol), 1)
    red = jnp.where(r == c, 1.0, 0.0).astype(BF16)
    r2 = lax.broadcasted_iota(jnp.int32, (ncol, width), 0)
    c2 = lax.broadcasted_iota(jnp.int32, (ncol, width), 1) // group
    exp = jnp.where(r2 == c2, 1.0, 0.0).astype(BF16)
    return red, exp


def _group_sum(x, red, exp):
    return _split_dot(_split_dot(x, red), exp)


def _norm_kernel(x_ref, g_ref, h_ref):
    x = x_ref[...]
    ms = jnp.mean(x * x, axis=-1, keepdims=True)
    h_ref[...] = (x * lax.rsqrt(ms + NORM_EPS) * g_ref[...]).astype(BF16)


def _norm_rows(x, g, tm):
    rows, d = x.shape
    return pl.pallas_call(
        _norm_kernel,
        grid=(rows // tm,),
        in_specs=[pl.BlockSpec((tm, d), lambda i: (i, 0)), pl.BlockSpec((1, d), lambda i: (0, 0))],
        out_specs=pl.BlockSpec((tm, d), lambda i: (i, 0)),
        out_shape=jax.ShapeDtypeStruct((rows, d), BF16),
        compiler_params=_cparams(("parallel",)),
    )(x, g)


def _mm_kernel(h_ref, w_ref, o_ref):
    o_ref[...] = jnp.dot(h_ref[...], w_ref[...], preferred_element_type=F32)


def _matmul(h, w, tm, tn):
    rows, k = h.shape
    n = w.shape[1]
    return pl.pallas_call(
        _mm_kernel,
        grid=(rows // tm, n // tn),
        in_specs=[pl.BlockSpec((tm, k), lambda i, j: (i, 0)), pl.BlockSpec((k, tn), lambda i, j: (0, j))],
        out_specs=pl.BlockSpec((tm, tn), lambda i, j: (i, j)),
        out_shape=jax.ShapeDtypeStruct((rows, n), F32),
        compiler_params=_cparams(("parallel", "arbitrary")),
    )(h, w)


def _rw_vectors(z, prev, mu_ref, w0_ref, wup_ref, a0_ref, aup_ref, kk_ref, ka_ref, rk_ref):
    zs = z + (prev - z) * mu_ref[...]
    r = zs[:, 0:RW_WIDTH]
    k = zs[:, RW_WIDTH:2 * RW_WIDTH]
    v = zs[:, 2 * RW_WIDTH:3 * RW_WIDTH]
    lora = zs[:, 3 * RW_WIDTH:RW_SHIFT_DIM]
    lane = lax.broadcasted_iota(jnp.int32, lora.shape, 1)
    lora = jnp.where(lane < RW_LORA, jnp.tanh(lora), lora).astype(BF16)
    wpre = w0_ref[...] + jnp.dot(lora, wup_ref[...], preferred_element_type=F32)
    apre = a0_ref[...] + jnp.dot(lora, aup_ref[...], preferred_element_type=F32)
    nw = -wpre
    softplus = jnp.maximum(nw, 0.0) + jnp.log1p(jnp.exp(-jnp.abs(nw)))
    logw = -jnp.exp(-softplus - 0.5)
    a = jax.nn.sigmoid(apre)
    red, exp = _group_mats(RW_WIDTH, RW_HEAD_DIM, LANES)
    kk = k * kk_ref[...]
    nrm = jnp.maximum(jnp.sqrt(_group_sum(kk * kk, red, exp)), 1e-12)
    kk = kk / nrm
    kmod = k * (1.0 + (a - 1.0) * ka_ref[...])
    bonus = _group_sum(r * kmod * rk_ref[...], red, exp) * v
    return r, logw, kmod, v, kk, kk * a, bonus


def _rw_prep_kernel(z_ref, prev_ref, mu_ref, w0_ref, wup_ref, a0_ref, aup_ref, kk_ref, ka_ref, rk_ref,
                    r_o, w_o, k_o, v_o, kk_o, kka_o, bonus_o):
    r, logw, kmod, v, kk, kka, bonus = _rw_vectors(
        z_ref[...], prev_ref[...], mu_ref, w0_ref, wup_ref, a0_ref, aup_ref, kk_ref, ka_ref, rk_ref)
    r_o[...] = r
    w_o[...] = jnp.exp(logw)
    k_o[...] = kmod
    v_o[...] = v
    kk_o[...] = kk
    kka_o[...] = kka
    bonus_o[...] = bonus


RW_CHUNK = 64


def _split3_dot(m, x):
    x0 = x.astype(BF16)
    r1 = x - x0.astype(F32)
    x1 = r1.astype(BF16)
    x2 = (r1 - x1.astype(F32)).astype(BF16)
    return (jnp.dot(m, x0, preferred_element_type=F32) + jnp.dot(m, x1, preferred_element_type=F32)
            + jnp.dot(m, x2, preferred_element_type=F32))


def _rw_prep_chunk_kernel(z_ref, zprev_ref, shift_ref, mu_ref, w0_ref, wup_ref, a0_ref, aup_ref, kk_ref, ka_ref,
                          rk_ref, a_o, r_o, kt_o, bt_o, kh_o, bh_o, v_o, pc_o, bonus_o, *, tiles_per_seq):
    i = pl.program_id(0)
    z = z_ref[...]
    tm = z.shape[0]
    first = jnp.where(i % tiles_per_seq == 0, shift_ref[...], zprev_ref[7:8, :])
    row = lax.broadcasted_iota(jnp.int32, z.shape, 0)
    prev = jnp.where(row == 0, first, pltpu.roll(z, 1, axis=0))
    r, logw, kmod, v, kk, kka, bonus = _rw_vectors(
        z, prev, mu_ref, w0_ref, wup_ref, a0_ref, aup_ref, kk_ref, ka_ref, rk_ref)
    tr = lax.broadcasted_iota(jnp.int32, (tm, tm), 0)
    tc = lax.broadcasted_iota(jnp.int32, (tm, tm), 1)
    tri = jnp.where((tr // RW_CHUNK == tc // RW_CHUNK) & (tc <= tr), 1.0, 0.0).astype(BF16)
    cum = _split3_dot(tri, logw)
    tot = []
    for c in range(tm // RW_CHUNK):
        last = cum[(c + 1) * RW_CHUNK - 1:(c + 1) * RW_CHUNK, :]
        pc_o[c] = jnp.exp(last)
        tot.append(jnp.broadcast_to(last, (RW_CHUNK, RW_WIDTH)))
    tot = jnp.concatenate(tot, axis=0)
    pinv = jnp.exp(-cum)
    pend = jnp.exp(tot - cum)
    a_o[...] = (kk * jnp.exp(cum - logw)).astype(BF16)
    r_o[...] = (r * jnp.exp(cum)).astype(BF16)
    kt_o[...] = (kmod * pinv).astype(BF16)
    bt_o[...] = (kka * pinv).astype(BF16)
    kh_o[...] = (kmod * pend).astype(BF16)
    bh_o[...] = (kka * pend).astype(BF16)
    v_o[...] = v.astype(BF16)
    bonus_o[...] = bonus


def _rw_prep_chunk(z, shift0, mu, w0, wup, a0, aup, k_k, k_a, r_k, tm, t):
    rows = z.shape[0]
    tiles_per_seq = t // tm
    row_in = pl.BlockSpec((tm, RW_SHIFT_DIM), lambda i: (i, 0))
    prev_in = pl.BlockSpec((8, RW_SHIFT_DIM), lambda i: (jnp.maximum(i * (tm // 8) - 1, 0), 0))
    shift_in = pl.BlockSpec((None, 1, RW_SHIFT_DIM), lambda i: (i // tiles_per_seq, 0, 0))
    vec = lambda n: pl.BlockSpec((1, n), lambda i: (0, 0))
    mat = pl.BlockSpec((2 * RW_LORA, RW_WIDTH), lambda i: (0, 0))
    row_out = pl.BlockSpec((tm, RW_WIDTH), lambda i: (i, 0))
    pc_out = pl.BlockSpec((tm // RW_CHUNK, 1, RW_WIDTH), lambda i: (i, 0, 0))
    bf = jax.ShapeDtypeStruct((rows, RW_WIDTH), BF16)
    return pl.pallas_call(
        functools.partial(_rw_prep_chunk_kernel, tiles_per_seq=tiles_per_seq),
        grid=(rows // tm,),
        in_specs=[row_in, prev_in, shift_in, vec(RW_SHIFT_DIM), vec(RW_WIDTH), mat, vec(RW_WIDTH), mat,
                  vec(RW_WIDTH), vec(RW_WIDTH), vec(RW_WIDTH)],
        out_specs=[row_out] * 7 + [pc_out, row_out],
        out_shape=[bf] * 7 + [jax.ShapeDtypeStruct((rows // RW_CHUNK, 1, RW_WIDTH), F32),
                              jax.ShapeDtypeStruct((rows, RW_WIDTH), F32)],
        compiler_params=_cparams(("parallel",)),
    )(z, z, shift0[:, None, :], mu, w0, wup, a0, aup, k_k, k_a, r_k)


RW_PAIRS = RW_HEADS // 2
PAIR = 2 * RW_HEAD_DIM
PSTK = 2 * RW_CHUNK


def _bdot(a, b):
    return jnp.dot(a.astype(BF16), b.astype(BF16), preferred_element_type=F32)


def _rw_chunk_kernel(a_ref, r_ref, kt_ref, bt_ref, kh_ref, bh_ref, v_ref, rp_o, oi_o, q_o, ds_o):
    lane_lo = lax.broadcasted_iota(jnp.int32, (RW_CHUNK, PAIR), 1) < RW_HEAD_DIM
    rr = lax.broadcasted_iota(jnp.int32, (PSTK, PSTK), 0)
    cc = lax.broadcasted_iota(jnp.int32, (PSTK, PSTK), 1)
    same = lambda b: (rr // b) == (cc // b)
    strict = same(RW_CHUNK) & (cc < rr)
    incl = same(RW_CHUNK) & (cc <= rr)
    eye = jnp.where(rr == cc, 1.0, 0.0)

    def stack(x):
        zero = jnp.zeros_like(x)
        return jnp.concatenate([jnp.where(lane_lo, x, zero), jnp.where(lane_lo, zero, x)], axis=0)

    prs = range(RW_PAIRS)
    sls = [slice(pr * PAIR, (pr + 1) * PAIR) for pr in prs]
    a_s, r_s, kt_s, bt_s, kh_s, bh_s, v_s = ([stack(ref[:, sl]) for sl in sls] for ref in
                                             (a_ref, r_ref, kt_ref, bt_ref, kh_ref, bh_ref, v_ref))
    g = [_nt(jnp.concatenate([a_s[i], r_s[i]], axis=0), jnp.concatenate([kt_s[i], bt_s[i]], axis=0)) for i in prs]
    m_ak = [jnp.where(strict, g[i][0:PSTK, 0:PSTK], 0.0).astype(BF16) for i in prs]
    m_ab = [jnp.where(strict, g[i][0:PSTK, PSTK:], 0.0) for i in prs]
    m_rk = [jnp.where(incl, g[i][PSTK:, 0:PSTK], 0.0).astype(BF16) for i in prs]
    m_rb = [jnp.where(incl, g[i][PSTK:, PSTK:], 0.0).astype(BF16) for i in prs]
    n1 = [jnp.where(same(8), -m_ab[i], 0.0) for i in prs]
    n2 = [_bdot(n1[i], n1[i]) for i in prs]
    n4 = [_bdot(n2[i], n2[i]) for i in prs]
    p = [_bdot(eye + n1[i], eye + n2[i]) for i in prs]
    p = [_bdot(p[i], eye + n4[i]) for i in prs]
    for b in (8, 16, 32):
        blk = same(2 * b) & jnp.logical_not(same(b))
        t = [_bdot(p[i], jnp.where(blk, m_ab[i], 0.0)) for i in prs]
        p = [p[i] - _bdot(t[i], p[i]) for i in prs]
    mv = [_bdot(m_ak[i], v_s[i]) for i in prs]
    aw = [_bdot(p[i], jnp.concatenate([a_s[i], mv[i].astype(BF16)], axis=1)) for i in prs]
    ap = [aw[i][:, 0:PAIR] for i in prs]
    w = [aw[i][:, PAIR:] for i in prs]
    rp = [r_s[i].astype(F32) - _bdot(m_rb[i], ap[i]) for i in prs]
    oi = [_bdot(jnp.concatenate([m_rk[i], -m_rb[i]], axis=1),
                jnp.concatenate([v_s[i], w[i].astype(BF16)], axis=0)) for i in prs]
    q = [_bdot(ap[i].T, bh_s[i]) for i in prs]
    ds = [_bdot(jnp.concatenate([v_s[i].astype(F32), -w[i]], axis=0).T,
                jnp.concatenate([kh_s[i], bh_s[i]], axis=0)) for i in prs]
    for i in prs:
        rp_o[:, sls[i]] = (rp[i][0:RW_CHUNK] + rp[i][RW_CHUNK:]).astype(BF16)
        oi_o[:, sls[i]] = oi[i][0:RW_CHUNK] + oi[i][RW_CHUNK:]
        q_o[i] = q[i].astype(BF16)
        ds_o[i] = ds[i]


def _rw_chunk(arrs):
    rows = arrs[0].shape[0]
    nchunk = rows // RW_CHUNK
    row = pl.BlockSpec((RW_CHUNK, RW_WIDTH), lambda n: (n, 0))
    mat = pl.BlockSpec((None, RW_PAIRS, PAIR, PAIR), lambda n: (n, 0, 0, 0))
    return pl.pallas_call(
        _rw_chunk_kernel,
        grid=(nchunk,),
        in_specs=[row] * 7,
        out_specs=[row, row, mat, mat],
        out_shape=[jax.ShapeDtypeStruct((rows, RW_WIDTH), BF16), jax.ShapeDtypeStruct((rows, RW_WIDTH), F32),
                   jax.ShapeDtypeStruct((nchunk, RW_PAIRS, PAIR, PAIR), BF16),
                   jax.ShapeDtypeStruct((nchunk, RW_PAIRS, PAIR, PAIR), F32)],
        compiler_params=_cparams(("parallel",)),
    )(*arrs)


def _rw_state_kernel(rp_ref, oi_ref, pc_ref, q_ref, ds_ref, o_ref, sfin_ref, s_scr):
    n = pl.program_id(1)

    @pl.when(n == 0)
    def _():
        s_scr[...] = jnp.zeros_like(s_scr)

    for pr in range(RW_PAIRS):
        sl = slice(pr * PAIR, (pr + 1) * PAIR)
        s = s_scr[pr]
        sb = s.astype(BF16)
        o_ref[:, sl] = oi_ref[:, sl] + _nt(rp_ref[:, sl], sb)
        s_scr[pr] = s * pc_ref[:, sl] - jnp.dot(sb, q_ref[pr], preferred_element_type=F32) + ds_ref[pr]

    @pl.when(n == pl.num_programs(1) - 1)
    def _():
        for pr in range(RW_PAIRS):
            s = s_scr[pr]
            sfin_ref[2 * pr] = s[0:RW_HEAD_DIM, 0:RW_HEAD_DIM]
            sfin_ref[2 * pr + 1] = s[RW_HEAD_DIM:, RW_HEAD_DIM:]


def _rw_state(rp, oi, pc, q, ds, nb, t):
    nchunk = t // RW_CHUNK
    row = pl.BlockSpec((RW_CHUNK, RW_WIDTH), lambda b, n: (b * nchunk + n, 0))
    pcs = pl.BlockSpec((None, 1, RW_WIDTH), lambda b, n: (b * nchunk + n, 0, 0))
    mat = pl.BlockSpec((None, RW_PAIRS, PAIR, PAIR), lambda b, n: (b * nchunk + n, 0, 0, 0))
    sspec = pl.BlockSpec((None, RW_HEADS, RW_HEAD_DIM, RW_HEAD_DIM), lambda b, n: (b, 0, 0, 0))
    return pl.pallas_call(
        _rw_state_kernel,
        grid=(nb, nchunk),
        in_specs=[row, row, pcs, mat, mat],
        out_specs=[row, sspec],
        out_shape=[jax.ShapeDtypeStruct((nb * t, RW_WIDTH), F32),
                   jax.ShapeDtypeStruct((nb, RW_HEADS, RW_HEAD_DIM, RW_HEAD_DIM), F32)],
        scratch_shapes=[pltpu.VMEM((RW_PAIRS, PAIR, PAIR), F32)],
        compiler_params=_cparams(("parallel", "arbitrary")),
    )(rp, oi, pc, q, ds)


def _rw_prep(z, prev, mu, w0, wup, a0, aup, k_k, k_a, r_k, tm):
    rows = z.shape[0]
    row_in = pl.BlockSpec((tm, RW_SHIFT_DIM), lambda i: (i, 0))
    vec = lambda n: pl.BlockSpec((1, n), lambda i: (0, 0))
    mat = pl.BlockSpec((2 * RW_LORA, RW_WIDTH), lambda i: (0, 0))
    row_out = pl.BlockSpec((tm, RW_WIDTH), lambda i: (i, 0))
    out = jax.ShapeDtypeStruct((rows, RW_WIDTH), F32)
    return pl.pallas_call(
        _rw_prep_kernel,
        grid=(rows // tm,),
        in_specs=[row_in, row_in, vec(RW_SHIFT_DIM), vec(RW_WIDTH), mat, vec(RW_WIDTH), mat,
                  vec(RW_WIDTH), vec(RW_WIDTH), vec(RW_WIDTH)],
        out_specs=[row_out] * 7,
        out_shape=[out] * 7,
        compiler_params=_cparams(("parallel",)),
    )(z, prev, mu, w0, wup, a0, aup, k_k, k_a, r_k)


SCAN_COLS = 128
SCAN_GROUP = 4


def _rw_step(st, cols, v_row):
    kkc, wc, kkac, kc, rc = cols
    sk = jnp.sum(st * kkc, axis=0, keepdims=True)
    st = st * wc - kkac * sk + kc * v_row
    o = jnp.sum(st * rc, axis=0, keepdims=True)
    return st, o


def _col(xt, c):
    return jnp.broadcast_to(xt[:, c:c + 1], (RW_HEAD_DIM, RW_HEAD_DIM))


def _rw_scan_sample_kernel(kk_ref, w_ref, kka_ref, k_ref, r_ref, v_ref, s0_ref, o_ref, sfin_ref, o_scr,
                           *, seg):
    for g in range(2):
        xts = [ref[g] for ref in (kk_ref, w_ref, kka_ref, k_ref, r_ref)]
        vg = v_ref[:, g * RW_HEAD_DIM:(g + 1) * RW_HEAD_DIM]
        for b0 in range(0, SCAN_COLS // seg, SCAN_GROUP):
            group = range(b0, b0 + SCAN_GROUP)
            sts = [s0_ref[bi, g] for bi in group]
            for s in range(seg):
                for n, bi in enumerate(group):
                    c = bi * seg + s
                    sts[n], o = _rw_step(sts[n], [_col(xt, c) for xt in xts], vg[c:c + 1, :])
                    o_scr[g, c:c + 1, :] = o
            for n, bi in enumerate(group):
                sfin_ref[bi, g] = sts[n]
    o_ref[...] = jnp.concatenate([o_scr[0], o_scr[1]], axis=1)


def _rw_scan_sample(xts, v, s0t, seg):
    rows = v.shape[0]
    nseg = SCAN_COLS // seg
    xspec = pl.BlockSpec((2, RW_HEAD_DIM, SCAN_COLS), lambda p, cb: (p, 0, cb))
    vspec = pl.BlockSpec((SCAN_COLS, 2 * RW_HEAD_DIM), lambda p, cb: (cb, p))
    sspec = pl.BlockSpec((nseg, 2, RW_HEAD_DIM, RW_HEAD_DIM), lambda p, cb: (cb, p, 0, 0))
    return pl.pallas_call(
        functools.partial(_rw_scan_sample_kernel, seg=seg),
        grid=(RW_HEADS // 2, rows // SCAN_COLS),
        in_specs=[xspec] * 5 + [vspec, sspec],
        out_specs=[vspec, sspec],
        out_shape=[jax.ShapeDtypeStruct((rows, RW_WIDTH), F32),
                   jax.ShapeDtypeStruct(s0t.shape, F32)],
        scratch_shapes=[pltpu.VMEM((2, SCAN_COLS, RW_HEAD_DIM), F32)],
        compiler_params=_cparams(("parallel", "parallel")),
    )(*xts, v, s0t)


def _mla_q_c_kr(z_ref, cs_ref, sn_ref, ckr_ref, qng_ref, wuq_ref, wuqs_ref, kvng_ref, qg_ref):
    z = z_ref[...]
    cq = z[:, 0:MLA_Q_RANK]
    cq = (cq * lax.rsqrt(jnp.mean(cq * cq, axis=-1, keepdims=True) + NORM_EPS) * qng_ref[...]).astype(BF16)
    q_pre = jnp.dot(cq, wuq_ref[...], preferred_element_type=F32)
    q_swp = jnp.dot(cq, wuqs_ref[...], preferred_element_type=F32)
    cs, sn, qg = cs_ref[...], sn_ref[...], qg_ref[...]
    qs = []
    for h in range(MLA_HEADS):
        sl = slice(h * MLA_QPAD, (h + 1) * MLA_QPAD)
        q = q_pre[:, sl] * cs + q_swp[:, sl] * sn
        ms = jnp.sum(q * q, axis=-1, keepdims=True) * (1.0 / MLA_QK)
        qs.append(q * lax.rsqrt(ms + NORM_EPS) * qg)
    c = z[:, MLA_Q_RANK:MLA_Q_RANK + MLA_KV_RANK]
    c = c * lax.rsqrt(jnp.mean(c * c, axis=-1, keepdims=True) + NORM_EPS) * kvng_ref[...]
    kr2 = z[:, MLA_Q_RANK + MLA_KV_RANK:] * ckr_ref[...]
    kr2 = kr2 + pltpu.roll(kr2, MLA_ROPE, axis=1)
    return qs, c, kr2


def _mla_prep_prompt_kernel(z_ref, cs_ref, sn_ref, ckr_ref, qng_ref, wuq_ref, wuqs_ref, kvng_ref, qg_ref,
                            wuk_ref, wuv_ref, kg_ref, q_o, c_o, kr_o, k_o, v_o):
    qs, c, kr2 = _mla_q_c_kr(z_ref, cs_ref, sn_ref, ckr_ref, qng_ref, wuq_ref, wuqs_ref, kvng_ref, qg_ref)
    for h in range(MLA_HEADS):
        q_o[:, h * MLA_QPAD:(h + 1) * MLA_QPAD] = (qs[h] * MLA_SCALE).astype(BF16)
    c_o[...] = c
    kr_o[...] = kr2[:, 0:MLA_ROPE]
    cb = c.astype(BF16)
    kn = jnp.dot(cb, wuk_ref[...], preferred_element_type=F32)
    v_o[...] = jnp.dot(cb, wuv_ref[...], preferred_element_type=F32).astype(BF16)
    lane = lax.broadcasted_iota(jnp.int32, kr2.shape, 1)
    krz = jnp.where(lane < MLA_ROPE, kr2, 0.0)
    kr_ss = jnp.sum(krz * krz, axis=-1, keepdims=True)
    kg = kg_ref[...]
    for h in range(MLA_HEADS):
        knh = kn[:, h * MLA_NOPE:(h + 1) * MLA_NOPE]
        ms = (jnp.sum(knh * knh, axis=-1, keepdims=True) + kr_ss) * (1.0 / MLA_QK)
        rs = lax.rsqrt(ms + NORM_EPS)
        k_o[:, h * MLA_QPAD:h * MLA_QPAD + MLA_NOPE] = (knh * rs * kg[:, 0:MLA_NOPE]).astype(BF16)
        k_o[:, h * MLA_QPAD + MLA_NOPE:(h + 1) * MLA_QPAD] = (krz * rs * kg[:, MLA_NOPE:]).astype(BF16)


def _mla_prep_sample_kernel(z_ref, cs_ref, sn_ref, ckr_ref, qng_ref, wuq_ref, wuqs_ref, kvng_ref, qg_ref,
                            wukt_ref, kg_ref, qa_o, qr_o, c_o, kr_o):
    qs, c, kr2 = _mla_q_c_kr(z_ref, cs_ref, sn_ref, ckr_ref, qng_ref, wuq_ref, wuqs_ref, kvng_ref, qg_ref)
    c_o[...] = c
    kr_o[...] = kr2[:, 0:MLA_ROPE]
    kg = kg_ref[...]
    for h in range(MLA_HEADS):
        qk = qs[h] * kg * MLA_SCALE
        qn = qk[:, 0:MLA_NOPE].astype(BF16)
        qa_o[:, h * MLA_KV_RANK:(h + 1) * MLA_KV_RANK] = jnp.dot(
            qn, wukt_ref[h * MLA_NOPE:(h + 1) * MLA_NOPE, :], preferred_element_type=F32).astype(BF16)
        qr_o[:, h * LANES:(h + 1) * LANES] = qk[:, MLA_NOPE:].astype(BF16)


def _mla_prep(z, tabs, lw, tm, prompt):
    rows = z.shape[0]
    cs, sn, ckr = tabs
    ntab = cs.shape[0] // tm
    row = lambda n: pl.BlockSpec((tm, n), lambda i: (i, 0))
    tab = lambda n: pl.BlockSpec((tm, n), lambda i: (i % ntab, 0))
    full = lambda a: pl.BlockSpec(a.shape, lambda i: (0,) * a.ndim)
    common = [z, cs, sn, ckr, lw['mla_q_norm_g'], lw['wuq'], lw['wuqs'], lw['mla_kv_norm_g'], lw['mla_qg']]
    common_specs = [row(z.shape[1]), tab(MLA_QPAD), tab(MLA_QPAD), tab(LANES)] + [full(a) for a in common[4:]]
    if prompt:
        extra = [lw['wuk'], lw['wuv'], lw['mla_kg']]
        outs = [(MLA_HEADS * MLA_QPAD, BF16), (MLA_KV_RANK, F32), (MLA_ROPE, F32),
                (MLA_HEADS * MLA_QPAD, BF16), (MLA_WIDTH, BF16)]
        body = _mla_prep_prompt_kernel
    else:
        extra = [lw['wukt'], lw['mla_kg']]
        outs = [(MLA_HEADS * MLA_KV_RANK, BF16), (MLA_HEADS * LANES, BF16), (MLA_KV_RANK, F32),
                (MLA_ROPE, F32)]
        body = _mla_prep_sample_kernel
    return pl.pallas_call(
        body,
        grid=(rows // tm,),
        in_specs=common_specs + [full(a) for a in extra],
        out_specs=[row(n) for n, _ in outs],
        out_shape=[jax.ShapeDtypeStruct((rows, n), dt) for n, dt in outs],
        compiler_params=_cparams(("parallel",)),
    )(*common, *extra)


def _df_prep_kernel(z_ref, qg_ref, kg_ref, q_o, k_o, v_o, kb_o, vb_o):
    z = z_ref[...]
    q = z[:, 0:DF_NQ]
    red, exp = _group_mats(DF_NQ, DF_QK, LANES)
    ms = _group_sum(q * q, red, exp) * (1.0 / DF_QK)
    q_o[...] = (q * lax.rsqrt(ms + NORM_EPS) * qg_ref[...] * DF_SCALE).astype(BF16)
    k = z[:, DF_NQ:DF_NQ + DF_KDIM]
    red, exp = _group_mats(DF_KDIM, DF_QK, LANES)
    ms = _group_sum(k * k, red, exp) * (1.0 / DF_QK)
    k = k * lax.rsqrt(ms + NORM_EPS) * kg_ref[...]
    v = z[:, DF_NQ + DF_KDIM:]
    k_o[...] = k
    v_o[...] = v
    kb_o[...] = k.astype(BF16)
    vb_o[...] = v.astype(BF16)


def _df_prep(z, qg, kg, tm):
    rows = z.shape[0]
    row = lambda n: pl.BlockSpec((tm, n), lambda i: (i, 0))
    full = lambda a: pl.BlockSpec(a.shape, lambda i: (0,) * a.ndim)
    outs = [(DF_NQ, BF16), (DF_KDIM, F32), (DF_V, F32), (DF_KDIM, BF16), (DF_V, BF16)]
    return pl.pallas_call(
        _df_prep_kernel,
        grid=(rows // tm,),
        in_specs=[row(DF_IN), full(qg), full(kg)],
        out_specs=[row(n) for n, _ in outs],
        out_shape=[jax.ShapeDtypeStruct((rows, n), dt) for n, dt in outs],
        compiler_params=_cparams(("parallel",)),
    )(z, qg, kg)


def _online_update(ss, vs, m_ref, l_ref, acc_ref):
    ps, alphas = [], []
    for idx, s in enumerate(ss):
        m_old = m_ref[idx]
        m_new = jnp.maximum(m_old, jnp.max(s, axis=-1, keepdims=True))
        alpha = jnp.exp(m_old - m_new)
        p = jnp.exp(s - m_new)
        l_ref[idx] = alpha * l_ref[idx] + jnp.sum(p, axis=-1, keepdims=True)
        m_ref[idx] = m_new
        ps.append(p.astype(BF16))
        alphas.append(alpha)
    pvs = [jnp.dot(p, v, preferred_element_type=F32) for p, v in zip(ps, vs)]
    for idx, pv in enumerate(pvs):
        acc_ref[idx] = alphas[idx] * acc_ref[idx] + pv


def _causal_mask(i, j, tq, tk):
    row = i * tq + lax.broadcasted_iota(jnp.int32, (tq, tk), 0)
    col = j * tk + lax.broadcasted_iota(jnp.int32, (tq, tk), 1)
    return col <= row


def _flash_init(m_ref, l_ref, acc_ref):
    m_ref[...] = jnp.full(m_ref.shape, NEG_BIG, F32)
    l_ref[...] = jnp.zeros_like(l_ref)
    acc_ref[...] = jnp.zeros_like(acc_ref)


def _mla_flash_kernel(q_ref, k_ref, v_ref, o_ref, m_ref, l_ref, acc_ref, *, tq, tk):
    i, j = pl.program_id(1), pl.program_id(2)

    @pl.when(j == 0)
    def _():
        _flash_init(m_ref, l_ref, acc_ref)

    @pl.when(j <= i)
    def _():
        mask = _causal_mask(i, j, tq, tk)
        sls = [slice(h * MLA_QPAD, (h + 1) * MLA_QPAD) for h in range(MLA_HEADS)]
        ss = [_nt(q_ref[:, sl], k_ref[:, sl]) for sl in sls]
        ss = [jnp.where(mask, s, NEG_BIG) for s in ss]
        vs = [v_ref[:, h * MLA_V:(h + 1) * MLA_V] for h in range(MLA_HEADS)]
        _online_update(ss, vs, m_ref, l_ref, acc_ref)

    @pl.when(j == i)
    def _():
        for h in range(MLA_HEADS):
            o_ref[:, h * MLA_V:(h + 1) * MLA_V] = acc_ref[h] / l_ref[h]


def _df_lambda(lq1, lk1, lq2, lk2, lam_init):
    return (jnp.exp(jnp.sum(lq1 * lk1, axis=-1, keepdims=True))
            - jnp.exp(jnp.sum(lq2 * lk2, axis=-1, keepdims=True)) + lam_init)


def _df_merge_norm(o1, o2, lam_refs, lamc_ref, sg_ref):
    lq1, lk1, lq2, lk2 = (r[...] for r in lam_refs)
    lam_init = lamc_ref[:, 0:1]
    lam = _df_lambda(lq1, lk1, lq2, lk2, lam_init)
    o = o1 - lam * o2
    o = o * lax.rsqrt(jnp.mean(o * o, axis=-1, keepdims=True) + NORM_EPS) * sg_ref[...]
    return o * (1.0 - lam_init)


def _df_flash_kernel(q_ref, k_ref, v_ref, lq1, lk1, lq2, lk2, lamc_ref, sg_ref, o_ref,
                     qm_ref, m_ref, l_ref, acc_ref, *, tq, tk):
    i, j = pl.program_id(1), pl.program_id(2)

    @pl.when(j == 0)
    def _():
        _flash_init(m_ref, l_ref, acc_ref)
        lane = lax.broadcasted_iota(jnp.int32, (tq, 2 * DF_QK), 1)
        for h in range(DF_HEADS):
            q = q_ref[:, h * 2 * DF_QK:(h + 1) * 2 * DF_QK]
            qm_ref[2 * h] = jnp.where(lane < DF_QK, q, jnp.zeros_like(q))
            qm_ref[2 * h + 1] = jnp.where(lane >= DF_QK, q, jnp.zeros_like(q))

    @pl.when(j <= i)
    def _():
        mask = _causal_mask(i, j, tq, tk)
        k, v = k_ref[...], v_ref[...]
        ss = [_nt(qm_ref[vh], k) for vh in range(2 * DF_HEADS)]
        ss = [jnp.where(mask, s, NEG_BIG) for s in ss]
        _online_update(ss, [v] * (2 * DF_HEADS), m_ref, l_ref, acc_ref)

    @pl.when(j == i)
    def _():
        for h in range(DF_HEADS):
            o1 = acc_ref[2 * h] / l_ref[2 * h]
            o2 = acc_ref[2 * h + 1] / l_ref[2 * h + 1]
            o_ref[:, h * DF_V:(h + 1) * DF_V] = _df_merge_norm(o1, o2, (lq1, lk1, lq2, lk2), lamc_ref, sg_ref)


def _flash_specs(nq, tq, tk, wq, wk, wv, wo):
    qspec = pl.BlockSpec((None, tq, wq), lambda b, i, j: (b, i, 0))
    kspec = pl.BlockSpec((None, tk, wk), lambda b, i, j: (b, jnp.minimum(i, j), 0))
    vspec = pl.BlockSpec((None, tk, wv), lambda b, i, j: (b, jnp.minimum(i, j), 0))
    ospec = pl.BlockSpec((None, tq, wo), lambda b, i, j: (b, i, 0))
    return qspec, kspec, vspec, ospec


def _mla_flash(q, k, v, tq):
    nb, t, _ = q.shape
    nq = t // tq
    qspec, kspec, vspec, ospec = _flash_specs(nq, tq, tq, q.shape[2], k.shape[2], v.shape[2], MLA_WIDTH)
    return pl.pallas_call(
        functools.partial(_mla_flash_kernel, tq=tq, tk=tq),
        grid=(nb, nq, nq),
        in_specs=[qspec, kspec, vspec],
        out_specs=ospec,
        out_shape=jax.ShapeDtypeStruct((nb, t, MLA_WIDTH), F32),
        scratch_shapes=[pltpu.VMEM((MLA_HEADS, tq, 1), F32), pltpu.VMEM((MLA_HEADS, tq, 1), F32),
                        pltpu.VMEM((MLA_HEADS, tq, MLA_V), F32)],
        compiler_params=_cparams(("parallel", "parallel", "arbitrary")),
    )(q, k, v)


def _df_flash(q, k, v, lam_args, tq):
    nb, t, _ = q.shape
    nq = t // tq
    qspec, kspec, vspec, ospec = _flash_specs(nq, tq, tq, q.shape[2], k.shape[2], v.shape[2], DF_WIDTH)
    small = [pl.BlockSpec(a.shape, lambda b, i, j: (0, 0)) for a in lam_args]
    nvh = 2 * DF_HEADS
    return pl.pallas_call(
        functools.partial(_df_flash_kernel, tq=tq, tk=tq),
        grid=(nb, nq, nq),
        in_specs=[qspec, kspec, vspec] + small,
        out_specs=ospec,
        out_shape=jax.ShapeDtypeStruct((nb, t, DF_WIDTH), F32),
        scratch_shapes=[pltpu.VMEM((nvh, tq, 2 * DF_QK), BF16), pltpu.VMEM((nvh, tq, 1), F32),
                        pltpu.VMEM((nvh, tq, 1), F32), pltpu.VMEM((nvh, tq, DF_V), F32)],
        compiler_params=_cparams(("parallel", "parallel", "arbitrary")),
    )(q, k, v, *lam_args)


NQ_MLA = 16
NQ_DF = 32
NEW_PAD = 128


def _paged_kernel(pt_ref, layer_ref,
                  qa_ref, qr_ref, qd_ref, cn_ref, krn_ref, kdn_ref, vdn_ref,
                  wukt_ref, wuv_ref, lq1, lk1, lq2, lk2, lamc_ref, sg_ref,
                  cc_hbm, ckr_hbm, ck_hbm, cv_hbm,
                  omla_ref, odf_ref,
                  cbuf, krbuf, kbuf, vbuf, sems, lhs_ref, m1, l1, acc1, m2, l2, acc2,
                  *, pages, nchunk, nstep):
    b = pl.program_id(0)
    nb = pl.num_programs(0)
    layer = layer_ref[0]
    lc = pages * PAGE_SIZE

    def copies(bb, ci, slot):
        out = []
        for p in range(pages):
            page = pt_ref[bb, ci * pages + p]
            win = pl.ds(p * PAGE_SIZE, PAGE_SIZE)
            for n, (hbm, buf) in enumerate(((cc_hbm, cbuf), (ck_hbm, kbuf), (cv_hbm, vbuf))):
                out.append(pltpu.make_async_copy(hbm.at[layer, page], buf.at[slot, win], sems.at[n, slot]))
            out.append(pltpu.make_async_copy(ckr_hbm.at[layer, page], krbuf.at[slot, :, win], sems.at[3, slot]))
        return out

    @pl.when(b == 0)
    def _():
        for cp in copies(0, 0, 0):
            cp.start()

    lhs_ref[0:MLA_HEADS * MLA_NOPE, :] = wukt_ref[...]
    lhs_ref[MLA_HEADS * MLA_NOPE:, :] = qa_ref[...]
    m1[...] = jnp.full(m1.shape, NEG_BIG, F32)
    l1[...] = jnp.zeros_like(l1)
    acc1[...] = jnp.zeros_like(acc1)
    m2[...] = jnp.full(m2.shape, NEG_BIG, F32)
    l2[...] = jnp.zeros_like(l2)
    acc2[...] = jnp.zeros_like(acc2)

    def weights(s, m_ref, l_ref):
        m_old = m_ref[...]
        m_new = jnp.maximum(m_old, jnp.max(s, axis=-1, keepdims=True))
        alpha = jnp.exp(m_old - m_new)
        p = jnp.exp(s - m_new)
        l_ref[...] = alpha * l_ref[...] + jnp.sum(p, axis=-1, keepdims=True)
        m_ref[...] = m_new
        return p.astype(BF16), alpha

    def attend(c, krt, kd, vd, mask):
        n = c.shape[0]
        big = _nt(lhs_ref[...], c)
        s_rope = jnp.dot(qr_ref[...], krt.astype(BF16), preferred_element_type=F32)
        sd = _nt(qd_ref[...], kd)
        kn = big[0:MLA_HEADS * MLA_NOPE]
        ss_n = jnp.sum((kn * kn).reshape(MLA_HEADS, MLA_NOPE, n), axis=1)
        ss_r = jnp.sum(krt * krt, axis=0, keepdims=True)
        rs = lax.rsqrt((ss_n + ss_r) * (1.0 / MLA_QK) + NORM_EPS)
        s = big[MLA_HEADS * MLA_NOPE:] + s_rope
        head = lax.broadcasted_iota(jnp.int32, (NQ_MLA, n), 0) // nstep
        rs_rows = jnp.zeros((NQ_MLA, n), F32)
        for h in range(MLA_HEADS):
            rs_rows = jnp.where(head == h, rs[h:h + 1, :], rs_rows)
        s = s * rs_rows
        if mask is not None:
            s = jnp.where(mask[0], s, NEG_BIG)
            sd = jnp.where(mask[1], sd, NEG_BIG)
        p1, alpha1 = weights(s, m1, l1)
        p2, alpha2 = weights(sd, m2, l2)
        pv1 = jnp.dot(p1, c, preferred_element_type=F32)
        pv2 = jnp.dot(p2, vd, preferred_element_type=F32)
        acc1[...] = alpha1 * acc1[...] + pv1
        acc2[...] = alpha2 * acc2[...] + pv2

    def chunk(ci, carry):
        slot = ci % 2
        last = ci == nchunk - 1
        nbb = jnp.where(last, b + 1, b)
        nci = jnp.where(last, 0, ci + 1)

        @pl.when(nbb < nb)
        def _():
            for cp in copies(nbb, nci, 1 - slot):
                cp.start()

        for cp in copies(b, ci, slot):
            cp.wait()
        attend(cbuf[slot].astype(BF16), krbuf[slot], kbuf[slot].astype(BF16), vbuf[slot].astype(BF16), None)
        return carry

    lax.fori_loop(0, nchunk, chunk, 0)

    key = lax.broadcasted_iota(jnp.int32, (NQ_MLA, NEW_PAD), 1)
    step = lax.broadcasted_iota(jnp.int32, (NQ_MLA, NEW_PAD), 0) % nstep
    keyd = lax.broadcasted_iota(jnp.int32, (NQ_DF, NEW_PAD), 1)
    stepd = lax.broadcasted_iota(jnp.int32, (NQ_DF, NEW_PAD), 0) % nstep
    attend(cn_ref[...], krn_ref[...], kdn_ref[...], vdn_ref[...], (key <= step, keyd <= stepd))

    o_abs = (acc1[...] / l1[...]).astype(BF16)
    full = jnp.dot(o_abs, wuv_ref[...], preferred_element_type=F32)
    head = lax.broadcasted_iota(jnp.int32, (NQ_MLA, MLA_V), 0) // nstep
    o = jnp.zeros((NQ_MLA, MLA_V), F32)
    for h in range(MLA_HEADS):
        o = jnp.where(head == h, full[:, h * MLA_V:(h + 1) * MLA_V], o)
    omla_ref[...] = o
    od = acc2[...] / l2[...]
    odf_ref[...] = _df_merge_norm(od[0:NQ_DF // 2], od[NQ_DF // 2:], (lq1, lk1, lq2, lk2), lamc_ref, sg_ref)


def _paged_attention(page_table, layer, qa, qr, qd, cn, krn, kdn, vdn, lw, caches, pages, nstep):
    nb, npages = page_table.shape
    nchunk = npages // pages
    lc = pages * PAGE_SIZE
    per_b = lambda a: pl.BlockSpec((None,) + a.shape[1:], lambda b, pt, ly: (b,) + (0,) * (a.ndim - 1))
    full = lambda a: pl.BlockSpec(a.shape, lambda b, pt, ly: (0,) * a.ndim)
    any_spec = pl.BlockSpec(memory_space=pl.ANY)
    batched = [qa, qr, qd, cn, krn, kdn, vdn]
    shared = [lw['wukt'], lw['wuv'], lw['df_lq1'], lw['df_lk1'], lw['df_lq2'], lw['df_lk2'], lw['lamc'],
              lw['df_subln_g']]
    ospec = pl.BlockSpec((None, NQ_MLA, LANES), lambda b, pt, ly: (b, 0, 0))
    grid_spec = pltpu.PrefetchScalarGridSpec(
        num_scalar_prefetch=2,
        grid=(nb,),
        in_specs=[per_b(a) for a in batched] + [full(a) for a in shared] + [any_spec] * 4,
        out_specs=[ospec, ospec],
        scratch_shapes=[pltpu.VMEM((2, lc, MLA_KV_RANK), F32), pltpu.VMEM((2, MLA_ROPE, lc), F32),
                        pltpu.VMEM((2, lc, DF_KDIM), F32), pltpu.VMEM((2, lc, DF_V), F32),
                        pltpu.SemaphoreType.DMA((4, 2)),
                        pltpu.VMEM((MLA_HEADS * MLA_NOPE + NQ_MLA, MLA_KV_RANK), BF16),
                        pltpu.VMEM((NQ_MLA, 1), F32), pltpu.VMEM((NQ_MLA, 1), F32),
                        pltpu.VMEM((NQ_MLA, MLA_KV_RANK), F32),
                        pltpu.VMEM((NQ_DF, 1), F32), pltpu.VMEM((NQ_DF, 1), F32),
                        pltpu.VMEM((NQ_DF, DF_V), F32)])
    return pl.pallas_call(
        functools.partial(_paged_kernel, pages=pages, nchunk=nchunk, nstep=nstep),
        grid_spec=grid_spec,
        out_shape=[jax.ShapeDtypeStruct((nb, NQ_MLA, MLA_V), F32), jax.ShapeDtypeStruct((nb, NQ_MLA, DF_V), F32)],
        compiler_params=_cparams(("arbitrary",)),
    )(page_table, layer, *batched, *shared, *caches)


def _out_kernel(x_ref, orw_ref, bonus_ref, omla_ref, odf_ref, g_ref, wout_ref, gng_ref, gnb_ref, lng_ref,
                xo_ref, h_ref):
    red, exp = _group_mats(RW_WIDTH, RW_HEAD_DIM, LANES)
    o = orw_ref[...]
    mean = _group_sum(o, red, exp) * (1.0 / RW_HEAD_DIM)
    d = o - mean
    var = _group_sum(d * d, red, exp) * (1.0 / RW_HEAD_DIM)
    y_rw = d * lax.rsqrt(var + GN_EPS) * gng_ref[...] + gnb_ref[...] + bonus_ref[...]
    g = g_ref[...]
    sg = g * jax.nn.sigmoid(g)
    a0, a1 = RW_WIDTH, RW_WIDTH + MLA_WIDTH
    acc = jnp.dot((y_rw * sg[:, 0:a0]).astype(BF16), wout_ref[0:a0, :], preferred_element_type=F32)
    acc += jnp.dot((omla_ref[...] * sg[:, a0:a1]).astype(BF16), wout_ref[a0:a1, :], preferred_element_type=F32)
    acc += jnp.dot((odf_ref[...] * sg[:, a1:]).astype(BF16), wout_ref[a1:, :], preferred_element_type=F32)
    xn = x_ref[...] + acc
    xo_ref[...] = xn
    ms = jnp.mean(xn * xn, axis=-1, keepdims=True)
    h_ref[...] = (xn * lax.rsqrt(ms + NORM_EPS) * lng_ref[...]).astype(BF16)


def _out_proj(x, o_rw, bonus, o_mla, o_df, gates, wout, gn_g, gn_b, ln_g_next, tm):
    rows = x.shape[0]
    row = lambda n: pl.BlockSpec((tm, n), lambda i: (i, 0))
    full = lambda a: pl.BlockSpec(a.shape, lambda i: (0,) * a.ndim)
    return pl.pallas_call(
        _out_kernel,
        grid=(rows // tm,),
        in_specs=[row(D_MODEL), row(RW_WIDTH), row(RW_WIDTH), row(MLA_WIDTH), row(DF_WIDTH), row(D_MIX),
                  full(wout), full(gn_g), full(gn_b), full(ln_g_next)],
        out_specs=[row(D_MODEL), row(D_MODEL)],
        out_shape=[jax.ShapeDtypeStruct((rows, D_MODEL), F32), jax.ShapeDtypeStruct((rows, D_MODEL), BF16)],
        compiler_params=_cparams(("parallel",)),
    )(x, o_rw, bonus, o_mla, o_df, gates, wout, gn_g, gn_b, ln_g_next)


def _prep_weights(P):
    depth = P['w_in'].shape[0]
    w_in = P['w_in']
    kr0 = OFF_MLA + MLA_Q_RANK + MLA_KV_RANK
    half = MLA_ROPE // 2
    kr_swapped = jnp.concatenate([w_in[:, :, kr0 + half:kr0 + MLA_ROPE], w_in[:, :, kr0:kr0 + half]], axis=-1)
    W = {
        'w_rw': w_in[:, :, 0:OFF_RW_G].astype(BF16),
        'w_gate': jnp.concatenate([w_in[:, :, OFF_RW_G:OFF_MLA], w_in[:, :, OFF_MLA_G:OFF_DF],
                                   w_in[:, :, OFF_DF_G:N_IN]], axis=-1).astype(BF16),
        'w_mla': jnp.concatenate([w_in[:, :, OFF_MLA:OFF_MLA_G], kr_swapped], axis=-1).astype(BF16),
        'w_df': w_in[:, :, OFF_DF:OFF_DF_G].astype(BF16),
        'w_out': P['w_out'].astype(BF16),
    }
    zl = jnp.zeros((depth, RW_LORA, RW_WIDTH), F32)
    W['wup'] = jnp.concatenate([P['rw_w_up'], zl], axis=1).astype(BF16)
    W['aup'] = jnp.concatenate([zl, P['rw_a_up']], axis=1).astype(BF16)
    uq = P['mla_w_uq'].reshape(depth, MLA_Q_RANK, MLA_HEADS, MLA_QK)
    nope, rope = uq[..., :MLA_NOPE], uq[..., MLA_NOPE:]
    rope_sw = jnp.concatenate([rope[..., half:], rope[..., :half]], axis=-1)
    z64 = jnp.zeros(rope.shape, F32)
    z128 = jnp.zeros(nope.shape, F32)
    W['wuq'] = jnp.concatenate([nope, rope, z64], axis=-1).reshape(depth, MLA_Q_RANK, -1).astype(BF16)
    W['wuqs'] = jnp.concatenate([z128, rope_sw, z64], axis=-1).reshape(depth, MLA_Q_RANK, -1).astype(BF16)
    ukv = P['mla_w_ukv'].reshape(depth, MLA_KV_RANK, MLA_HEADS, MLA_NOPE + MLA_V)
    wuk = ukv[..., :MLA_NOPE].reshape(depth, MLA_KV_RANK, -1)
    W['wuk'] = wuk.astype(BF16)
    W['wukt'] = jnp.swapaxes(wuk, 1, 2).astype(BF16)
    W['wuv'] = ukv[..., MLA_NOPE:].reshape(depth, MLA_KV_RANK, -1).astype(BF16)
    pad = jnp.zeros((depth, MLA_QPAD - MLA_QK), F32)
    W['mla_qg'] = jnp.concatenate([P['mla_qn_g'], pad], axis=-1)
    W['mla_kg'] = jnp.concatenate([P['mla_kn_g'], pad], axis=-1)
    W['df_qg'] = jnp.tile(P['df_qn_g'], (1, DF_NQ // DF_QK))
    W['df_kg'] = jnp.tile(P['df_kn_g'], (1, DF_KDIM // DF_QK))
    W['rw_r_k'] = P['rw_r_k'].reshape(depth, RW_WIDTH)
    lam_init = jnp.asarray([0.8 - 0.6 * math.exp(-0.3 * i) for i in range(depth)], F32)
    W['lamc'] = jnp.broadcast_to(lam_init[:, None], (depth, LANES))
    for name in ('ln_g', 'rw_mu', 'rw_w0', 'rw_a0', 'rw_k_k', 'rw_k_a', 'rw_gn_g', 'rw_gn_b', 'mla_q_norm_g',
                 'mla_kv_norm_g', 'df_lq1', 'df_lk1', 'df_lq2', 'df_lk2', 'df_subln_g'):
        W[name] = P[name]
    return W


def _layer_weights(W, i):
    out = {}
    for name, a in W.items():
        out[name] = a[i] if a.ndim == 3 else a[i][None, :]
    return out


def _rope_tables(pos):
    half = MLA_ROPE // 2
    inv = ROPE_THETA ** (-jnp.arange(half, dtype=F32) / half)
    ang = pos.astype(F32)[:, None] * inv[None, :]
    cos, sin = jnp.cos(ang), jnp.sin(ang)
    n = pos.shape[0]
    cc = jnp.concatenate([cos, cos], axis=-1)
    ss = jnp.concatenate([-sin, sin], axis=-1)
    cs = jnp.concatenate([jnp.ones((n, MLA_NOPE), F32), cc, jnp.zeros((n, MLA_QPAD - MLA_QK), F32)], axis=-1)
    sn = jnp.concatenate([jnp.zeros((n, MLA_NOPE), F32), ss, jnp.zeros((n, MLA_QPAD - MLA_QK), F32)], axis=-1)
    ckr = jnp.concatenate([cc, ss], axis=-1)
    return cs, sn, ckr


def _pick(n, prefs):
    for p in prefs:
        if n % p == 0:
            return p
    return n


def _trunk(x, W, past, depth):
    nb, t, _ = x.shape
    rows = nb * t
    prompt = past is None
    tm = _pick(rows, (1024, 512, 256, 128))
    tm_small = _pick(rows, (512, 256, 128))
    tm_out = _pick(rows, (256, 128))
    tm_rw = _pick(rows, (256, 128))
    if prompt:
        pos = jnp.arange(t)
        tabs = _rope_tables(pos)
        tq = _pick(t, (512, 256, 128))
    else:
        npages = past['page_table'].shape[1]
        pos = npages * PAGE_SIZE + jnp.arange(t)
        tabs = tuple(jnp.tile(a, (nb, 1)) for a in _rope_tables(pos))
        pages = _pick(npages, (32, 16, 8, 4, 2, 1))
        if npages // pages < 2 or (npages // pages) % 2:
            pages = max(1, pages // 2)
    xf = x.reshape(rows, D_MODEL)
    h = _norm_rows(xf, W['ln_g'][0][None, :], tm_small)
    new = {k: [] for k in ('mla_latent', 'mla_krope', 'diff_k', 'diff_v', 'rwkv_state', 'rwkv_shift')}
    for i in range(depth):
        lw = _layer_weights(W, i)
        z_rw = _matmul(h, lw['w_rw'], tm, _pick(RW_SHIFT_DIM, (640,)))
        gates = _matmul(h, lw['w_gate'], tm, _pick(D_MIX, (1024,)))
        z_mla = _matmul(h, lw['w_mla'], tm, lw['w_mla'].shape[1])
        z_df = _matmul(h, lw['w_df'], tm, DF_IN)

        z3 = z_rw.reshape(nb, t, RW_SHIFT_DIM)
        shift0 = jnp.zeros((nb, RW_SHIFT_DIM), F32) if prompt else past['rwkv_shift'][i]
        rw_args = (lw['rw_mu'], lw['rw_w0'], lw['wup'], lw['rw_a0'], lw['aup'],
                   lw['rw_k_k'], lw['rw_k_a'], lw['rw_r_k'])
        if prompt:
            *folded, pc, bonus = _rw_prep_chunk(z_rw, shift0, *rw_args, tm_rw, t)
            rp, oi, q, ds = _rw_chunk(folded)
            o_rw, st = _rw_state(rp, oi, pc, q, ds, nb, t)
        else:
            prev = jnp.concatenate([shift0[:, None, :], z3[:, :-1]], axis=1).reshape(rows, RW_SHIFT_DIM)
            r, w, kmod, v, kk, kka, bonus = _rw_prep(z_rw, prev, *rw_args, tm_rw)
            xts = [a.reshape(rows, RW_HEADS, RW_HEAD_DIM).transpose(1, 2, 0) for a in (kk, w, kka, kmod, r)]
            s0t = jnp.swapaxes(past['rwkv_state'][i], -1, -2)
            o_rw, st = _rw_scan_sample(xts, v, s0t, t)
            st = jnp.swapaxes(st, -1, -2)
        new['rwkv_state'].append(st)
        new['rwkv_shift'].append(z3[:, -1])

        qd, kd, vd, kdb, vdb = _df_prep(z_df, lw['df_qg'], lw['df_kg'], tm_small)
        lam_args = [lw['df_lq1'], lw['df_lk1'], lw['df_lq2'], lw['df_lk2'], lw['lamc'], lw['df_subln_g']]
        if prompt:
            q, c, kr, k, vm = _mla_prep(z_mla, tabs, lw, tm_small, True)
            o_mla = _mla_flash(q.reshape(nb, t, -1), k.reshape(nb, t, -1), vm.reshape(nb, t, -1), tq)
            o_df = _df_flash(qd.reshape(nb, t, -1), kdb.reshape(nb, t, -1), vdb.reshape(nb, t, -1), lam_args, tq)
            o_mla = o_mla.reshape(rows, MLA_WIDTH)
            o_df = o_df.reshape(rows, DF_WIDTH)
        else:
            qa, qr, c, kr = _mla_prep(z_mla, tabs, lw, tm_small, False)
            qa = qa.reshape(nb, t, MLA_HEADS, MLA_KV_RANK).transpose(0, 2, 1, 3).reshape(nb, -1, MLA_KV_RANK)
            qr = qr.reshape(nb, t, MLA_HEADS, LANES)[..., :MLA_ROPE].transpose(0, 2, 1, 3)
            qr = qr.reshape(nb, -1, MLA_ROPE)
            q5 = qd.reshape(nb, t, DF_HEADS, 2, DF_QK).transpose(0, 3, 2, 1, 4)
            zq = jnp.zeros_like(q5[:, 0])
            qdm = jnp.stack([jnp.concatenate([q5[:, 0], zq], axis=-1),
                             jnp.concatenate([zq, q5[:, 1]], axis=-1)], axis=1).reshape(nb, -1, 2 * DF_QK)
            padk = lambda a: jnp.pad(a.reshape(nb, t, -1), ((0, 0), (0, NEW_PAD - t), (0, 0)))
            layer = jnp.full((1,), i, jnp.int32)
            caches = (past['mla_latent'], past['mla_krope_t'], past['diff_k'], past['diff_v'])
            krn = jnp.pad(jnp.swapaxes(kr.reshape(nb, t, MLA_ROPE), 1, 2), ((0, 0), (0, 0), (0, NEW_PAD - t)))
            o_mla, o_df = _paged_attention(
                past['page_table'], layer, qa, qr, qdm, padk(c).astype(BF16), krn, padk(kdb), padk(vdb),
                lw, caches, pages, t)
            o_mla = o_mla.reshape(nb, MLA_HEADS, t, MLA_V).transpose(0, 2, 1, 3).reshape(rows, MLA_WIDTH)
            o_df = o_df.reshape(nb, DF_HEADS, t, DF_V).transpose(0, 2, 1, 3).reshape(rows, DF_WIDTH)
        new['mla_latent'].append(c.reshape(nb, t, MLA_KV_RANK))
        new['mla_krope'].append(kr.reshape(nb, t, MLA_ROPE))
        new['diff_k'].append(kd.reshape(nb, t, DF_KDIM))
        new['diff_v'].append(vd.reshape(nb, t, DF_V))

        ln_next = W['ln_g'][min(i + 1, depth - 1)][None, :]
        xf, h = _out_proj(xf, o_rw, bonus, o_mla, o_df, gates, lw['w_out'], lw['rw_gn_g'], lw['rw_gn_b'],
                          ln_next, tm_out)
    return xf.reshape(nb, t, D_MODEL), {k: jnp.stack(vs) for k, vs in new.items()}


def kernel(x_prompt, x_sample, cache_mla_latent, cache_mla_krope, cache_diff_k, cache_diff_v, state_rwkv, state_rwkv_shift, page_table, ln_g, w_in, w_out, rw_mu, rw_w0, rw_w_up, rw_a0, rw_a_up, rw_k_k, rw_k_a, rw_r_k, rw_gn_g, rw_gn_b, mla_q_norm_g, mla_w_uq, mla_kv_norm_g, mla_w_ukv, mla_qn_g, mla_kn_g, df_qn_g, df_kn_g, df_lq1, df_lk1, df_lq2, df_lk2, df_subln_g):
    P = dict(ln_g=ln_g, w_in=w_in, w_out=w_out, rw_mu=rw_mu, rw_w0=rw_w0, rw_w_up=rw_w_up,
             rw_a0=rw_a0, rw_a_up=rw_a_up, rw_k_k=rw_k_k, rw_k_a=rw_k_a, rw_r_k=rw_r_k,
             rw_gn_g=rw_gn_g, rw_gn_b=rw_gn_b, mla_q_norm_g=mla_q_norm_g, mla_w_uq=mla_w_uq,
             mla_kv_norm_g=mla_kv_norm_g, mla_w_ukv=mla_w_ukv, mla_qn_g=mla_qn_g,
             mla_kn_g=mla_kn_g, df_qn_g=df_qn_g, df_kn_g=df_kn_g, df_lq1=df_lq1, df_lk1=df_lk1,
             df_lq2=df_lq2, df_lk2=df_lk2, df_subln_g=df_subln_g)
    depth = w_in.shape[0]
    W = _prep_weights(P)
    y_p, sp = _trunk(x_prompt, W, None, depth)
    past = dict(mla_latent=cache_mla_latent, mla_krope_t=jnp.swapaxes(cache_mla_krope, 2, 3), diff_k=cache_diff_k,
                diff_v=cache_diff_v, rwkv_state=state_rwkv, rwkv_shift=state_rwkv_shift,
                page_table=page_table)
    y_s, ss = _trunk(x_sample, W, past, depth)
    keys = ('mla_latent', 'mla_krope', 'diff_k', 'diff_v', 'rwkv_state', 'rwkv_shift')
    return (y_p, y_s) + tuple(sp[k] for k in keys) + tuple(ss[k] for k in keys)
```

```python
import functools
import math

import jax
import jax.numpy as jnp
from jax import lax
from jax.experimental import pallas as pl
from jax.experimental.pallas import tpu as pltpu

F32 = jnp.float32
BF16 = jnp.bfloat16

D_MODEL = 2048
PAGE_SIZE = 128
RW_HEADS = 16
RW_HEAD_DIM = 64
RW_WIDTH = RW_HEADS * RW_HEAD_DIM
RW_LORA = 64
RW_SHIFT_DIM = 3 * RW_WIDTH + 2 * RW_LORA
MLA_HEADS = 4
MLA_NOPE = 128
MLA_ROPE = 64
MLA_QK = MLA_NOPE + MLA_ROPE
MLA_V = 128
MLA_WIDTH = MLA_HEADS * MLA_V
MLA_Q_RANK = 384
MLA_KV_RANK = 256
MLA_IN = MLA_Q_RANK + MLA_KV_RANK + MLA_ROPE
MLA_QPAD = 256
ROPE_THETA = 10000.0
MLA_SCALE = MLA_QK ** -0.5
DF_HEADS = 4
DF_QK = 64
DF_V = 128
DF_WIDTH = DF_HEADS * DF_V
DF_NQ = DF_HEADS * 2 * DF_QK
DF_KDIM = 2 * DF_QK
DF_IN = DF_NQ + DF_KDIM + DF_V
DF_SCALE = DF_QK ** -0.5
D_MIX = RW_WIDTH + MLA_WIDTH + DF_WIDTH
OFF_RW_G = RW_SHIFT_DIM
OFF_MLA = OFF_RW_G + RW_WIDTH
OFF_MLA_G = OFF_MLA + MLA_IN
OFF_DF = OFF_MLA_G + MLA_WIDTH
OFF_DF_G = OFF_DF + DF_IN
N_IN = OFF_DF_G + DF_WIDTH
NORM_EPS = 1e-6
GN_EPS = 64e-5
NEG_BIG = -1e30

LANES = 128
VMEM_LIMIT = 56 * 1024 * 1024


def _cparams(sem):
    return pltpu.CompilerParams(dimension_semantics=sem, vmem_limit_bytes=VMEM_LIMIT)


def _nt(a, b):
    return lax.dot_general(a, b, (((1,), (1,)), ((), ())), preferred_element_type=F32)


def _split_dot(x, g):
    hi = x.astype(BF16)
    lo = (x - hi.astype(F32)).astype(BF16)
    return (jnp.dot(hi, g, preferred_element_type=F32) + jnp.dot(lo, g, preferred_element_type=F32))


def _group_mats(width, group, ncol):
    r = lax.broadcasted_iota(jnp.int32, (width, ncol), 0) // group
    c = lax.broadcasted_iota(jnp.int32, (width, ncol), 1)
    red = jnp.where(r == c, 1.0, 0.0).astype(BF16)
    r2 = lax.broadcasted_iota(jnp.int32, (ncol, width), 0)
    c2 = lax.broadcasted_iota(jnp.int32, (ncol, width), 1) // group
    exp = jnp.where(r2 == c2, 1.0, 0.0).astype(BF16)
    return red, exp


def _group_sum(x, red, exp):
    return _split_dot(_split_dot(x, red), exp)


def _norm_kernel(x_ref, g_ref, h_ref):
    x = x_ref[...]
    ms = jnp.mean(x * x, axis=-1, keepdims=True)
    h_ref[...] = (x * lax.rsqrt(ms + NORM_EPS) * g_ref[...]).astype(BF16)


def _norm_rows(x, g, tm):
    rows, d = x.shape
    return pl.pallas_call(
        _norm_kernel,
        grid=(rows // tm,),
        in_specs=[pl.BlockSpec((tm, d), lambda i: (i, 0)), pl.BlockSpec((1, d), lambda i: (0, 0))],
        out_specs=pl.BlockSpec((tm, d), lambda i: (i, 0)),
        out_shape=jax.ShapeDtypeStruct((rows, d), BF16),
        compiler_params=_cparams(("parallel",)),
    )(x, g)


def _mm_kernel(h_ref, w_ref, o_ref):
    o_ref[...] = jnp.dot(h_ref[...], w_ref[...], preferred_element_type=F32)


def _matmul(h, w, tm, tn):
    rows, k = h.shape
    n = w.shape[1]
    return pl.pallas_call(
        _mm_kernel,
        grid=(rows // tm, n // tn),
        in_specs=[pl.BlockSpec((tm, k), lambda i, j: (i, 0)), pl.BlockSpec((k, tn), lambda i, j: (0, j))],
        out_specs=pl.BlockSpec((tm, tn), lambda i, j: (i, j)),
        out_shape=jax.ShapeDtypeStruct((rows, n), F32),
        compiler_params=_cparams(("parallel", "arbitrary")),
    )(h, w)


def _rw_vectors(z, prev, mu_ref, w0_ref, wup_ref, a0_ref, aup_ref, kk_ref, ka_ref, rk_ref):
    zs = z + (prev - z) * mu_ref[...]
    r = zs[:, 0:RW_WIDTH]
    k = zs[:, RW_WIDTH:2 * RW_WIDTH]
    v = zs[:, 2 * RW_WIDTH:3 * RW_WIDTH]
    lora = zs[:, 3 * RW_WIDTH:RW_SHIFT_DIM]
    lane = lax.broadcasted_iota(jnp.int32, lora.shape, 1)
    lora = jnp.where(lane < RW_LORA, jnp.tanh(lora), lora).astype(BF16)
    wpre = w0_ref[...] + jnp.dot(lora, wup_ref[...], preferred_element_type=F32)
    apre = a0_ref[...] + jnp.dot(lora, aup_ref[...], preferred_element_type=F32)
    nw = -wpre
    softplus = jnp.maximum(nw, 0.0) + jnp.log1p(jnp.exp(-jnp.abs(nw)))
    logw = -jnp.exp(-softplus - 0.5)
    a = jax.nn.sigmoid(apre)
    red, exp = _group_mats(RW_WIDTH, RW_HEAD_DIM, LANES)
    kk = k * kk_ref[...]
    nrm = jnp.maximum(jnp.sqrt(_group_sum(kk * kk, red, exp)), 1e-12)
    kk = kk / nrm
    kmod = k * (1.0 + (a - 1.0) * ka_ref[...])
    bonus = _group_sum(r * kmod * rk_ref[...], red, exp) * v
    return r, logw, kmod, v, kk, kk * a, bonus


def _rw_prep_kernel(z_ref, prev_ref, mu_ref, w0_ref, wup_ref, a0_ref, aup_ref, kk_ref, ka_ref, rk_ref,
                    r_o, w_o, k_o, v_o, kk_o, kka_o, bonus_o):
    r, logw, kmod, v, kk, kka, bonus = _rw_vectors(
        z_ref[...], prev_ref[...], mu_ref, w0_ref, wup_ref, a0_ref, aup_ref, kk_ref, ka_ref, rk_ref)
    r_o[...] = r
    w_o[...] = jnp.exp(logw)
    k_o[...] = kmod
    v_o[...] = v
    kk_o[...] = kk
    kka_o[...] = kka
    bonus_o[...] = bonus


RW_CHUNK = 64


def _split3_dot(m, x):
    x0 = x.astype(BF16)
    r1 = x - x0.astype(F32)
    x1 = r1.astype(BF16)
    x2 = (r1 - x1.astype(F32)).astype(BF16)
    return (jnp.dot(m, x0, preferred_element_type=F32) + jnp.dot(m, x1, preferred_element_type=F32)
            + jnp.dot(m, x2, preferred_element_type=F32))


def _rw_prep_chunk_kernel(z_ref, zprev_ref, shift_ref, mu_ref, w0_ref, wup_ref, a0_ref, aup_ref, kk_ref, ka_ref,
                          rk_ref, a_o, r_o, kt_o, bt_o, kh_o, bh_o, v_o, pc_o, bonus_o, *, tiles_per_seq):
    i = pl.program_id(0)
    z = z_ref[...]
    tm = z.shape[0]
    first = jnp.where(i % tiles_per_seq == 0, shift_ref[...], zprev_ref[7:8, :])
    row = lax.broadcasted_iota(jnp.int32, z.shape, 0)
    prev = jnp.where(row == 0, first, pltpu.roll(z, 1, axis=0))
    r, logw, kmod, v, kk, kka, bonus = _rw_vectors(
        z, prev, mu_ref, w0_ref, wup_ref, a0_ref, aup_ref, kk_ref, ka_ref, rk_ref)
    tr = lax.broadcasted_iota(jnp.int32, (tm, tm), 0)
    tc = lax.broadcasted_iota(jnp.int32, (tm, tm), 1)
    tri = jnp.where((tr // RW_CHUNK == tc // RW_CHUNK) & (tc <= tr), 1.0, 0.0).astype(BF16)
    cum = _split3_dot(tri, logw)
    tot = []
    for c in range(tm // RW_CHUNK):
        last = cum[(c + 1) * RW_CHUNK - 1:(c + 1) * RW_CHUNK, :]
        pc_o[c] = jnp.exp(last)
        tot.append(jnp.broadcast_to(last, (RW_CHUNK, RW_WIDTH)))
    tot = jnp.concatenate(tot, axis=0)
    pinv = jnp.exp(-cum)
    pend = jnp.exp(tot - cum)
    a_o[...] = (kk * jnp.exp(cum - logw)).astype(BF16)
    r_o[...] = (r * jnp.exp(cum)).astype(BF16)
    kt_o[...] = (kmod * pinv).astype(BF16)
    bt_o[...] = (kka * pinv).astype(BF16)
    kh_o[...] = (kmod * pend).astype(BF16)
    bh_o[...] = (kka * pend).astype(BF16)
    v_o[...] = v.astype(BF16)
    bonus_o[...] = bonus


def _rw_prep_chunk(z, shift0, mu, w0, wup, a0, aup, k_k, k_a, r_k, tm, t):
    rows = z.shape[0]
    tiles_per_seq = t // tm
    row_in = pl.BlockSpec((tm, RW_SHIFT_DIM), lambda i: (i, 0))
    prev_in = pl.BlockSpec((8, RW_SHIFT_DIM), lambda i: (jnp.maximum(i * (tm // 8) - 1, 0), 0))
    shift_in = pl.BlockSpec((None, 1, RW_SHIFT_DIM), lambda i: (i // tiles_per_seq, 0, 0))
    vec = lambda n: pl.BlockSpec((1, n), lambda i: (0, 0))
    mat = pl.BlockSpec((2 * RW_LORA, RW_WIDTH), lambda i: (0, 0))
    row_out = pl.BlockSpec((tm, RW_WIDTH), lambda i: (i, 0))
    pc_out = pl.BlockSpec((tm // RW_CHUNK, 1, RW_WIDTH), lambda i: (i, 0, 0))
    bf = jax.ShapeDtypeStruct((rows, RW_WIDTH), BF16)
    return pl.pallas_call(
        functools.partial(_rw_prep_chunk_kernel, tiles_per_seq=tiles_per_seq),
        grid=(rows // tm,),
        in_specs=[row_in, prev_in, shift_in, vec(RW_SHIFT_DIM), vec(RW_WIDTH), mat, vec(RW_WIDTH), mat,
                  vec(RW_WIDTH), vec(RW_WIDTH), vec(RW_WIDTH)],
        out_specs=[row_out] * 7 + [pc_out, row_out],
        out_shape=[bf] * 7 + [jax.ShapeDtypeStruct((rows // RW_CHUNK, 1, RW_WIDTH), F32),
                              jax.ShapeDtypeStruct((rows, RW_WIDTH), F32)],
        compiler_params=_cparams(("parallel",)),
    )(z, z, shift0[:, None, :], mu, w0, wup, a0, aup, k_k, k_a, r_k)


RW_PAIRS = RW_HEADS // 2
PAIR = 2 * RW_HEAD_DIM
PSTK = 2 * RW_CHUNK


def _bdot(a, b):
    return jnp.dot(a.astype(BF16), b.astype(BF16), preferred_element_type=F32)


def _rw_chunk_kernel(a_ref, r_ref, kt_ref, bt_ref, kh_ref, bh_ref, v_ref, rp_o, oi_o, q_o, ds_o):
    lane_lo = lax.broadcasted_iota(jnp.int32, (RW_CHUNK, PAIR), 1) < RW_HEAD_DIM
    rr = lax.broadcasted_iota(jnp.int32, (PSTK, PSTK), 0)
    cc = lax.broadcasted_iota(jnp.int32, (PSTK, PSTK), 1)
    same = lambda b: (rr // b) == (cc // b)
    strict = same(RW_CHUNK) & (cc < rr)
    incl = same(RW_CHUNK) & (cc <= rr)
    eye = jnp.where(rr == cc, 1.0, 0.0)

    def stack(x):
        zero = jnp.zeros_like(x)
        return jnp.concatenate([jnp.where(lane_lo, x, zero), jnp.where(lane_lo, zero, x)], axis=0)

    prs = range(RW_PAIRS)
    sls = [slice(pr * PAIR, (pr + 1) * PAIR) for pr in prs]
    a_s, r_s, kt_s, bt_s, kh_s, bh_s, v_s = ([stack(ref[:, sl]) for sl in sls] for ref in
                                             (a_ref, r_ref, kt_ref, bt_ref, kh_ref, bh_ref, v_ref))
    g = [_nt(jnp.concatenate([a_s[i], r_s[i]], axis=0), jnp.concatenate([kt_s[i], bt_s[i]], axis=0)) for i in prs]
    m_ak = [jnp.where(strict, g[i][0:PSTK, 0:PSTK], 0.0).astype(BF16) for i in prs]
    m_ab = [jnp.where(strict, g[i][0:PSTK, PSTK:], 0.0) for i in prs]
    m_rk = [jnp.where(incl, g[i][PSTK:, 0:PSTK], 0.0).astype(BF16) for i in prs]
    m_rb = [jnp.where(incl, g[i][PSTK:, PSTK:], 0.0).astype(BF16) for i in prs]
    n1 = [jnp.where(same(8), -m_ab[i], 0.0) for i in prs]
    n2 = [_bdot(n1[i], n1[i]) for i in prs]
    n4 = [_bdot(n2[i], n2[i]) for i in prs]
    p = [_bdot(eye + n1[i], eye + n2[i]) for i in prs]
    p = [_bdot(p[i], eye + n4[i]) for i in prs]
    for b in (8, 16, 32):
        blk = same(2 * b) & jnp.logical_not(same(b))
        t = [_bdot(p[i], jnp.where(blk, m_ab[i], 0.0)) for i in prs]
        p = [p[i] - _bdot(t[i], p[i]) for i in prs]
    mv = [_bdot(m_ak[i], v_s[i]) for i in prs]
    aw = [_bdot(p[i], jnp.concatenate([a_s[i], mv[i].astype(BF16)], axis=1)) for i in prs]
    ap = [aw[i][:, 0:PAIR] for i in prs]
    w = [aw[i][:, PAIR:] for i in prs]
    rp = [r_s[i].astype(F32) - _bdot(m_rb[i], ap[i]) for i in prs]
    oi = [_bdot(jnp.concatenate([m_rk[i], -m_rb[i]], axis=1),
                jnp.concatenate([v_s[i], w[i].astype(BF16)], axis=0)) for i in prs]
    q = [_bdot(ap[i].T, bh_s[i]) for i in prs]
    ds = [_bdot(jnp.concatenate([v_s[i].astype(F32), -w[i]], axis=0).T,
                jnp.concatenate([kh_s[i], bh_s[i]], axis=0)) for i in prs]
    for i in prs:
        rp_o[:, sls[i]] = (rp[i][0:RW_CHUNK] + rp[i][RW_CHUNK:]).astype(BF16)
        oi_o[:, sls[i]] = oi[i][0:RW_CHUNK] + oi[i][RW_CHUNK:]
        q_o[i] = q[i].astype(BF16)
        ds_o[i] = ds[i]


def _rw_chunk(arrs):
    rows = arrs[0].shape[0]
    nchunk = rows // RW_CHUNK
    row = pl.BlockSpec((RW_CHUNK, RW_WIDTH), lambda n: (n, 0))
    mat = pl.BlockSpec((None, RW_PAIRS, PAIR, PAIR), lambda n: (n, 0, 0, 0))
    return pl.pallas_call(
        _rw_chunk_kernel,
        grid=(nchunk,),
        in_specs=[row] * 7,
        out_specs=[row, row, mat, mat],
        out_shape=[jax.ShapeDtypeStruct((rows, RW_WIDTH), BF16), jax.ShapeDtypeStruct((rows, RW_WIDTH), F32),
                   jax.ShapeDtypeStruct((nchunk, RW_PAIRS, PAIR, PAIR), BF16),
                   jax.ShapeDtypeStruct((nchunk, RW_PAIRS, PAIR, PAIR), F32)],
        compiler_params=_cparams(("parallel",)),
    )(*arrs)


def _rw_state_kernel(rp_ref, oi_ref, pc_ref, q_ref, ds_ref, o_ref, sfin_ref, s_scr):
    n = pl.program_id(1)

    @pl.when(n == 0)
    def _():
        s_scr[...] = jnp.zeros_like(s_scr)

    for pr in range(RW_PAIRS):
        sl = slice(pr * PAIR, (pr + 1) * PAIR)
        s = s_scr[pr]
        sb = s.astype(BF16)
        o_ref[:, sl] = oi_ref[:, sl] + _nt(rp_ref[:, sl], sb)
        s_scr[pr] = s * pc_ref[:, sl] - jnp.dot(sb, q_ref[pr], preferred_element_type=F32) + ds_ref[pr]

    @pl.when(n == pl.num_programs(1) - 1)
    def _():
        for pr in range(RW_PAIRS):
            s = s_scr[pr]
            sfin_ref[2 * pr] = s[0:RW_HEAD_DIM, 0:RW_HEAD_DIM]
            sfin_ref[2 * pr + 1] = s[RW_HEAD_DIM:, RW_HEAD_DIM:]


def _rw_state(rp, oi, pc, q, ds, nb, t):
    nchunk = t // RW_CHUNK
    row = pl.BlockSpec((RW_CHUNK, RW_WIDTH), lambda b, n: (b * nchunk + n, 0))
    pcs = pl.BlockSpec((None, 1, RW_WIDTH), lambda b, n: (b * nchunk + n, 0, 0))
    mat = pl.BlockSpec((None, RW_PAIRS, PAIR, PAIR), lambda b, n: (b * nchunk + n, 0, 0, 0))
    sspec = pl.BlockSpec((None, RW_HEADS, RW_HEAD_DIM, RW_HEAD_DIM), lambda b, n: (b, 0, 0, 0))
    return pl.pallas_call(
        _rw_state_kernel,
        grid=(nb, nchunk),
        in_specs=[row, row, pcs, mat, mat],
        out_specs=[row, sspec],
        out_shape=[jax.ShapeDtypeStruct((nb * t, RW_WIDTH), F32),
                   jax.ShapeDtypeStruct((nb, RW_HEADS, RW_HEAD_DIM, RW_HEAD_DIM), F32)],
        scratch_shapes=[pltpu.VMEM((RW_PAIRS, PAIR, PAIR), F32)],
        compiler_params=_cparams(("parallel", "arbitrary")),
    )(rp, oi, pc, q, ds)


def _rw_prep(z, prev, mu, w0, wup, a0, aup, k_k, k_a, r_k, tm):
    rows = z.shape[0]
    row_in = pl.BlockSpec((tm, RW_SHIFT_DIM), lambda i: (i, 0))
    vec = lambda n: pl.BlockSpec((1, n), lambda i: (0, 0))
    mat = pl.BlockSpec((2 * RW_LORA, RW_WIDTH), lambda i: (0, 0))
    row_out = pl.BlockSpec((tm, RW_WIDTH), lambda i: (i, 0))
    out = jax.ShapeDtypeStruct((rows, RW_WIDTH), F32)
    return pl.pallas_call(
        _rw_prep_kernel,
        grid=(rows // tm,),
        in_specs=[row_in, row_in, vec(RW_SHIFT_DIM), vec(RW_WIDTH), mat, vec(RW_WIDTH), mat,
                  vec(RW_WIDTH), vec(RW_WIDTH), vec(RW_WIDTH)],
        out_specs=[row_out] * 7,
        out_shape=[out] * 7,
        compiler_params=_cparams(("parallel",)),
    )(z, prev, mu, w0, wup, a0, aup, k_k, k_a, r_k)


SCAN_COLS = 128
SCAN_GROUP = 8


def _col(xt, c):
    return jnp.broadcast_to(xt[:, c:c + 1], (RW_HEAD_DIM, RW_HEAD_DIM))


def _rw_scan_sample_kernel(kkt_ref, rt_ref, vt_ref, kka_ref, k_ref, w_ref, s0_ref, ot_ref, sfin_ref, *, seg):
    lane = lax.broadcasted_iota(jnp.int32, (RW_HEAD_DIM, SCAN_COLS), 1)
    for g in range(2):
        wts = jnp.concatenate([kkt_ref[g], rt_ref[g]], axis=1).astype(BF16)
        vt = vt_ref[g]
        ot = jnp.zeros((RW_HEAD_DIM, SCAN_COLS), F32)
        for b0 in range(0, SCAN_COLS // seg, SCAN_GROUP):
            group = list(range(b0, b0 + SCAN_GROUP))
            sts = [s0_ref[bi, g] for bi in group]
            for step in range(seg + 1):
                res = [jnp.dot(st.astype(BF16), wts, preferred_element_type=F32) for st in sts]
                for n, bi in enumerate(group):
                    if step > 0:
                        c = bi * seg + step - 1
                        ot = jnp.where(lane == c, res[n][:, SCAN_COLS:], ot)
                    if step < seg:
                        c = bi * seg + step
                        sk = _col(res[n][:, 0:SCAN_COLS], c)
                        sts[n] = (sts[n] * w_ref[g, c:c + 1, :] - sk * kka_ref[g, c:c + 1, :]
                                  + _col(vt, c) * k_ref[g, c:c + 1, :])
            for n, bi in enumerate(group):
                sfin_ref[bi, g] = sts[n]
        ot_ref[g] = ot


def _rw_scan_sample(cols, rows_, s0, seg):
    n = cols[0].shape[2]
    nseg = SCAN_COLS // seg
    cspec = pl.BlockSpec((2, RW_HEAD_DIM, SCAN_COLS), lambda p, cb: (p, 0, cb))
    rspec = pl.BlockSpec((2, SCAN_COLS, RW_HEAD_DIM), lambda p, cb: (p, cb, 0))
    sspec = pl.BlockSpec((nseg, 2, RW_HEAD_DIM, RW_HEAD_DIM), lambda p, cb: (cb, p, 0, 0))
    return pl.pallas_call(
        functools.partial(_rw_scan_sample_kernel, seg=seg),
        grid=(RW_HEADS // 2, n // SCAN_COLS),
        in_specs=[cspec] * 3 + [rspec] * 3 + [sspec],
        out_specs=[cspec, sspec],
        out_shape=[jax.ShapeDtypeStruct((RW_HEADS, RW_HEAD_DIM, n), F32),
                   jax.ShapeDtypeStruct(s0.shape, F32)],
        compiler_params=_cparams(("parallel", "parallel")),
    )(*cols, *rows_, s0)


def _mla_q_c_kr(z_ref, cs_ref, sn_ref, ckr_ref, qng_ref, wuq_ref, wuqs_ref, kvng_ref, qg_ref):
    z = z_ref[...]
    cq = z[:, 0:MLA_Q_RANK]
    cq = (cq * lax.rsqrt(jnp.mean(cq * cq, axis=-1, keepdims=True) + NORM_EPS) * qng_ref[...]).astype(BF16)
    q_pre = jnp.dot(cq, wuq_ref[...], preferred_element_type=F32)
    q_swp = jnp.dot(cq, wuqs_ref[...], preferred_element_type=F32)
    cs, sn, qg = cs_ref[...], sn_ref[...], qg_ref[...]
    qs = []
    for h in range(MLA_HEADS):
        sl = slice(h * MLA_QPAD, (h + 1) * MLA_QPAD)
        q = q_pre[:, sl] * cs + q_swp[:, sl] * sn
        ms = jnp.sum(q * q, axis=-1, keepdims=True) * (1.0 / MLA_QK)
        qs.append(q * lax.rsqrt(ms + NORM_EPS) * qg)
    c = z[:, MLA_Q_RANK:MLA_Q_RANK + MLA_KV_RANK]
    c = c * lax.rsqrt(jnp.mean(c * c, axis=-1, keepdims=True) + NORM_EPS) * kvng_ref[...]
    kr2 = z[:, MLA_Q_RANK + MLA_KV_RANK:] * ckr_ref[...]
    kr2 = kr2 + pltpu.roll(kr2, MLA_ROPE, axis=1)
    return qs, c, kr2


def _mla_prep_prompt_kernel(z_ref, cs_ref, sn_ref, ckr_ref, qng_ref, wuq_ref, wuqs_ref, kvng_ref, qg_ref,
                            wuk_ref, wuv_ref, kg_ref, q_o, c_o, kr_o, k_o, v_o):
    qs, c, kr2 = _mla_q_c_kr(z_ref, cs_ref, sn_ref, ckr_ref, qng_ref, wuq_ref, wuqs_ref, kvng_ref, qg_ref)
    for h in range(MLA_HEADS):
        q_o[:, h * MLA_QPAD:(h + 1) * MLA_QPAD] = (qs[h] * MLA_SCALE).astype(BF16)
    c_o[...] = c
    kr_o[...] = kr2[:, 0:MLA_ROPE]
    cb = c.astype(BF16)
    kn = jnp.dot(cb, wuk_ref[...], preferred_element_type=F32)
    v_o[...] = jnp.dot(cb, wuv_ref[...], preferred_element_type=F32).astype(BF16)
    lane = lax.broadcasted_iota(jnp.int32, kr2.shape, 1)
    krz = jnp.where(lane < MLA_ROPE, kr2, 0.0)
    kr_ss = jnp.sum(krz * krz, axis=-1, keepdims=True)
    kg = kg_ref[...]
    for h in range(MLA_HEADS):
        knh = kn[:, h * MLA_NOPE:(h + 1) * MLA_NOPE]
        ms = (jnp.sum(knh * knh, axis=-1, keepdims=True) + kr_ss) * (1.0 / MLA_QK)
        rs = lax.rsqrt(ms + NORM_EPS)
        k_o[:, h * MLA_QPAD:h * MLA_QPAD + MLA_NOPE] = (knh * rs * kg[:, 0:MLA_NOPE]).astype(BF16)
        k_o[:, h * MLA_QPAD + MLA_NOPE:(h + 1) * MLA_QPAD] = (krz * rs * kg[:, MLA_NOPE:]).astype(BF16)


def _mla_prep_sample_kernel(z_ref, cs_ref, sn_ref, ckr_ref, qng_ref, wuq_ref, wuqs_ref, kvng_ref, qg_ref,
                            wukt_ref, kg_ref, qa_o, qr_o, c_o, kr_o):
    qs, c, kr2 = _mla_q_c_kr(z_ref, cs_ref, sn_ref, ckr_ref, qng_ref, wuq_ref, wuqs_ref, kvng_ref, qg_ref)
    c_o[...] = c
    kr_o[...] = kr2[:, 0:MLA_ROPE]
    kg = kg_ref[...]
    for h in range(MLA_HEADS):
        qk = qs[h] * kg * MLA_SCALE
        qn = qk[:, 0:MLA_NOPE].astype(BF16)
        qa_o[:, h * MLA_KV_RANK:(h + 1) * MLA_KV_RANK] = jnp.dot(
            qn, wukt_ref[h * MLA_NOPE:(h + 1) * MLA_NOPE, :], preferred_element_type=F32).astype(BF16)
        qr_o[:, h * LANES:(h + 1) * LANES] = qk[:, MLA_NOPE:].astype(BF16)


def _mla_prep(z, tabs, lw, tm, prompt):
    rows = z.shape[0]
    cs, sn, ckr = tabs
    ntab = cs.shape[0] // tm
    row = lambda n: pl.BlockSpec((tm, n), lambda i: (i, 0))
    tab = lambda n: pl.BlockSpec((tm, n), lambda i: (i % ntab, 0))
    full = lambda a: pl.BlockSpec(a.shape, lambda i: (0,) * a.ndim)
    common = [z, cs, sn, ckr, lw['mla_q_norm_g'], lw['wuq'], lw['wuqs'], lw['mla_kv_norm_g'], lw['mla_qg']]
    common_specs = [row(z.shape[1]), tab(MLA_QPAD), tab(MLA_QPAD), tab(LANES)] + [full(a) for a in common[4:]]
    if prompt:
        extra = [lw['wuk'], lw['wuv'], lw['mla_kg']]
        outs = [(MLA_HEADS * MLA_QPAD, BF16), (MLA_KV_RANK, F32), (MLA_ROPE, F32),
                (MLA_HEADS * MLA_QPAD, BF16), (MLA_WIDTH, BF16)]
        body = _mla_prep_prompt_kernel
    else:
        extra = [lw['wukt'], lw['mla_kg']]
        outs = [(MLA_HEADS * MLA_KV_RANK, BF16), (MLA_HEADS * LANES, BF16), (MLA_KV_RANK, F32),
                (MLA_ROPE, F32)]
        body = _mla_prep_sample_kernel
    return pl.pallas_call(
        body,
        grid=(rows // tm,),
        in_specs=common_specs + [full(a) for a in extra],
        out_specs=[row(n) for n, _ in outs],
        out_shape=[jax.ShapeDtypeStruct((rows, n), dt) for n, dt in outs],
        compiler_params=_cparams(("parallel",)),
    )(*common, *extra)


def _df_prep_kernel(z_ref, qg_ref, kg_ref, q_o, k_o, v_o, kb_o, vb_o):
    z = z_ref[...]
    q = z[:, 0:DF_NQ]
    red, exp = _group_mats(DF_NQ, DF_QK, LANES)
    ms = _group_sum(q * q, red, exp) * (1.0 / DF_QK)
    q_o[...] = (q * lax.rsqrt(ms + NORM_EPS) * qg_ref[...] * DF_SCALE).astype(BF16)
    k = z[:, DF_NQ:DF_NQ + DF_KDIM]
    red, exp = _group_mats(DF_KDIM, DF_QK, LANES)
    ms = _group_sum(k * k, red, exp) * (1.0 / DF_QK)
    k = k * lax.rsqrt(ms + NORM_EPS) * kg_ref[...]
    v = z[:, DF_NQ + DF_KDIM:]
    k_o[...] = k
    v_o[...] = v
    kb_o[...] = k.astype(BF16)
    vb_o[...] = v.astype(BF16)


def _df_prep(z, qg, kg, tm):
    rows = z.shape[0]
    row = lambda n: pl.BlockSpec((tm, n), lambda i: (i, 0))
    full = lambda a: pl.BlockSpec(a.shape, lambda i: (0,) * a.ndim)
    outs = [(DF_NQ, BF16), (DF_KDIM, F32), (DF_V, F32), (DF_KDIM, BF16), (DF_V, BF16)]
    return pl.pallas_call(
        _df_prep_kernel,
        grid=(rows // tm,),
        in_specs=[row(DF_IN), full(qg), full(kg)],
        out_specs=[row(n) for n, _ in outs],
        out_shape=[jax.ShapeDtypeStruct((rows, n), dt) for n, dt in outs],
        compiler_params=_cparams(("parallel",)),
    )(z, qg, kg)


def _online_update(ss, vs, m_ref, l_ref, acc_ref):
    ps, alphas = [], []
    for idx, s in enumerate(ss):
        m_old = m_ref[idx]
        m_new = jnp.maximum(m_old, jnp.max(s, axis=-1, keepdims=True))
        alpha = jnp.exp(m_old - m_new)
        p = jnp.exp(s - m_new)
        l_ref[idx] = alpha * l_ref[idx] + jnp.sum(p, axis=-1, keepdims=True)
        m_ref[idx] = m_new
        ps.append(p.astype(BF16))
        alphas.append(alpha)
    pvs = [jnp.dot(p, v, preferred_element_type=F32) for p, v in zip(ps, vs)]
    for idx, pv in enumerate(pvs):
        acc_ref[idx] = alphas[idx] * acc_ref[idx] + pv


def _causal_mask(i, j, tq, tk):
    row = i * tq + lax.broadcasted_iota(jnp.int32, (tq, tk), 0)
    col = j * tk + lax.broadcasted_iota(jnp.int32, (tq, tk), 1)
    return col <= row


def _flash_init(m_ref, l_ref, acc_ref):
    m_ref[...] = jnp.full(m_ref.shape, NEG_BIG, F32)
    l_ref[...] = jnp.zeros_like(l_ref)
    acc_ref[...] = jnp.zeros_like(acc_ref)


def _mla_flash_kernel(q_ref, k_ref, v_ref, o_ref, m_ref, l_ref, acc_ref, *, tq, tk):
    i, j = pl.program_id(1), pl.program_id(2)

    @pl.when(j == 0)
    def _():
        _flash_init(m_ref, l_ref, acc_ref)

    def block(diagonal):
        sls = [slice(h * MLA_QPAD, (h + 1) * MLA_QPAD) for h in range(MLA_HEADS)]
        ss = [_nt(q_ref[:, sl], k_ref[:, sl]) for sl in sls]
        if diagonal:
            mask = _causal_mask(i, j, tq, tk)
            ss = [jnp.where(mask, s, NEG_BIG) for s in ss]
        vs = [v_ref[:, h * MLA_V:(h + 1) * MLA_V] for h in range(MLA_HEADS)]
        _online_update(ss, vs, m_ref, l_ref, acc_ref)

    pl.when(j < i)(functools.partial(block, False))
    pl.when(j == i)(functools.partial(block, True))

    @pl.when(j == i)
    def _():
        for h in range(MLA_HEADS):
            o_ref[:, h * MLA_V:(h + 1) * MLA_V] = acc_ref[h] / l_ref[h]


def _df_lambda(lq1, lk1, lq2, lk2, lam_init):
    return (jnp.exp(jnp.sum(lq1 * lk1, axis=-1, keepdims=True))
            - jnp.exp(jnp.sum(lq2 * lk2, axis=-1, keepdims=True)) + lam_init)


def _df_merge_norm(o1, o2, lam_refs, lamc_ref, sg_ref):
    lq1, lk1, lq2, lk2 = (r[...] for r in lam_refs)
    lam_init = lamc_ref[:, 0:1]
    lam = _df_lambda(lq1, lk1, lq2, lk2, lam_init)
    o = o1 - lam * o2
    o = o * lax.rsqrt(jnp.mean(o * o, axis=-1, keepdims=True) + NORM_EPS) * sg_ref[...]
    return o * (1.0 - lam_init)


def _df_flash_kernel(q_ref, k_ref, v_ref, lq1, lk1, lq2, lk2, lamc_ref, sg_ref, o_ref,
                     qm_ref, m_ref, l_ref, acc_ref, *, tq, tk):
    i, j = pl.program_id(1), pl.program_id(2)

    @pl.when(j == 0)
    def _():
        _flash_init(m_ref, l_ref, acc_ref)
        lane = lax.broadcasted_iota(jnp.int32, (tq, 2 * DF_QK), 1)
        for h in range(DF_HEADS):
            q = q_ref[:, h * 2 * DF_QK:(h + 1) * 2 * DF_QK]
            qm_ref[2 * h] = jnp.where(lane < DF_QK, q, jnp.zeros_like(q))
            qm_ref[2 * h + 1] = jnp.where(lane >= DF_QK, q, jnp.zeros_like(q))

    def block(diagonal):
        k, v = k_ref[...], v_ref[...]
        ss = [_nt(qm_ref[vh], k) for vh in range(2 * DF_HEADS)]
        if diagonal:
            mask = _causal_mask(i, j, tq, tk)
            ss = [jnp.where(mask, s, NEG_BIG) for s in ss]
        _online_update(ss, [v] * (2 * DF_HEADS), m_ref, l_ref, acc_ref)

    pl.when(j < i)(functools.partial(block, False))
    pl.when(j == i)(functools.partial(block, True))

    @pl.when(j == i)
    def _():
        for h in range(DF_HEADS):
            o1 = acc_ref[2 * h] / l_ref[2 * h]
            o2 = acc_ref[2 * h + 1] / l_ref[2 * h + 1]
            o_ref[:, h * DF_V:(h + 1) * DF_V] = _df_merge_norm(o1, o2, (lq1, lk1, lq2, lk2), lamc_ref, sg_ref)


def _flash_specs(nq, tq, tk, wq, wk, wv, wo):
    qspec = pl.BlockSpec((None, tq, wq), lambda b, i, j: (b, i, 0))
    kspec = pl.BlockSpec((None, tk, wk), lambda b, i, j: (b, jnp.minimum(i, j), 0))
    vspec = pl.BlockSpec((None, tk, wv), lambda b, i, j: (b, jnp.minimum(i, j), 0))
    ospec = pl.BlockSpec((None, tq, wo), lambda b, i, j: (b, i, 0))
    return qspec, kspec, vspec, ospec


def _mla_flash(q, k, v, tq):
    nb, t, _ = q.shape
    nq = t // tq
    qspec, kspec, vspec, ospec = _flash_specs(nq, tq, tq, q.shape[2], k.shape[2], v.shape[2], MLA_WIDTH)
    return pl.pallas_call(
        functools.partial(_mla_flash_kernel, tq=tq, tk=tq),
        grid=(nb, nq, nq),
        in_specs=[qspec, kspec, vspec],
        out_specs=ospec,
        out_shape=jax.ShapeDtypeStruct((nb, t, MLA_WIDTH), F32),
        scratch_shapes=[pltpu.VMEM((MLA_HEADS, tq, 1), F32), pltpu.VMEM((MLA_HEADS, tq, 1), F32),
                        pltpu.VMEM((MLA_HEADS, tq, MLA_V), F32)],
        compiler_params=_cparams(("parallel", "parallel", "arbitrary")),
    )(q, k, v)


def _df_flash(q, k, v, lam_args, tq):
    nb, t, _ = q.shape
    nq = t // tq
    qspec, kspec, vspec, ospec = _flash_specs(nq, tq, tq, q.shape[2], k.shape[2], v.shape[2], DF_WIDTH)
    small = [pl.BlockSpec(a.shape, lambda b, i, j: (0, 0)) for a in lam_args]
    nvh = 2 * DF_HEADS
    return pl.pallas_call(
        functools.partial(_df_flash_kernel, tq=tq, tk=tq),
        grid=(nb, nq, nq),
        in_specs=[qspec, kspec, vspec] + small,
        out_specs=ospec,
        out_shape=jax.ShapeDtypeStruct((nb, t, DF_WIDTH), F32),
        scratch_shapes=[pltpu.VMEM((nvh, tq, 2 * DF_QK), BF16), pltpu.VMEM((nvh, tq, 1), F32),
                        pltpu.VMEM((nvh, tq, 1), F32), pltpu.VMEM((nvh, tq, DF_V), F32)],
        compiler_params=_cparams(("parallel", "parallel", "arbitrary")),
    )(q, k, v, *lam_args)


NQ_MLA = 16
NQ_DF = 32
NEW_PAD = 128


def _paged_kernel(pt_ref, layer_ref,
                  qa_ref, qr_ref, qd_ref, cn_ref, krn_ref, kdn_ref, vdn_ref,
                  wukt_ref, wuv_ref, lq1, lk1, lq2, lk2, lamc_ref, sg_ref,
                  cc_hbm, ckr_hbm, ck_hbm, cv_hbm,
                  omla_ref, odf_ref,
                  cbuf, krbuf, kbuf, vbuf, sems, lhs_ref, m1, l1, acc1, m2, l2, acc2,
                  *, pages, nchunk, nstep):
    b = pl.program_id(0)
    nb = pl.num_programs(0)
    layer = layer_ref[0]
    lc = pages * PAGE_SIZE

    def copies(bb, ci, slot):
        out = []
        for p in range(pages):
            page = pt_ref[bb, ci * pages + p]
            win = pl.ds(p * PAGE_SIZE, PAGE_SIZE)
            for n, (hbm, buf) in enumerate(((cc_hbm, cbuf), (ck_hbm, kbuf), (cv_hbm, vbuf))):
                out.append(pltpu.make_async_copy(hbm.at[layer, page], buf.at[slot, win], sems.at[n, slot]))
            out.append(pltpu.make_async_copy(ckr_hbm.at[layer, page], krbuf.at[slot, :, win], sems.at[3, slot]))
        return out

    @pl.when(b == 0)
    def _():
        for cp in copies(0, 0, 0):
            cp.start()

    lhs_ref[0:MLA_HEADS * MLA_NOPE, :] = wukt_ref[...]
    lhs_ref[MLA_HEADS * MLA_NOPE:, :] = qa_ref[...]
    m1[...] = jnp.full(m1.shape, NEG_BIG, F32)
    l1[...] = jnp.zeros_like(l1)
    acc1[...] = jnp.zeros_like(acc1)
    m2[...] = jnp.full(m2.shape, NEG_BIG, F32)
    l2[...] = jnp.zeros_like(l2)
    acc2[...] = jnp.zeros_like(acc2)

    def weights(s, m_ref, l_ref):
        m_old = m_ref[...]
        m_new = jnp.maximum(m_old, jnp.max(s, axis=-1, keepdims=True))
        alpha = jnp.exp(m_old - m_new)
        p = jnp.exp(s - m_new)
        l_ref[...] = alpha * l_ref[...] + jnp.sum(p, axis=-1, keepdims=True)
        m_ref[...] = m_new
        return p.astype(BF16), alpha

    def attend(c, krt, kd, vd, mask):
        n = c.shape[0]
        big = _nt(lhs_ref[...], c)
        s_rope = jnp.dot(qr_ref[...], krt.astype(BF16), preferred_element_type=F32)
        sd = _nt(qd_ref[...], kd)
        kn = big[0:MLA_HEADS * MLA_NOPE]
        ss_n = jnp.sum((kn * kn).reshape(MLA_HEADS, MLA_NOPE, n), axis=1)
        ss_r = jnp.sum(krt * krt, axis=0, keepdims=True)
        rs = lax.rsqrt((ss_n + ss_r) * (1.0 / MLA_QK) + NORM_EPS)
        s = big[MLA_HEADS * MLA_NOPE:] + s_rope
        head = lax.broadcasted_iota(jnp.int32, (NQ_MLA, n), 0) // nstep
        rs_rows = jnp.zeros((NQ_MLA, n), F32)
        for h in range(MLA_HEADS):
            rs_rows = jnp.where(head == h, rs[h:h + 1, :], rs_rows)
        s = s * rs_rows
        if mask is not None:
            s = jnp.where(mask[0], s, NEG_BIG)
            sd = jnp.where(mask[1], sd, NEG_BIG)
        p1, alpha1 = weights(s, m1, l1)
        p2, alpha2 = weights(sd, m2, l2)
        pv1 = jnp.dot(p1, c, preferred_element_type=F32)
        pv2 = jnp.dot(p2, vd, preferred_element_type=F32)
        acc1[...] = alpha1 * acc1[...] + pv1
        acc2[...] = alpha2 * acc2[...] + pv2

    def chunk(ci, carry):
        slot = ci % 2
        last = ci == nchunk - 1
        nbb = jnp.where(last, b + 1, b)
        nci = jnp.where(last, 0, ci + 1)

        @pl.when(nbb < nb)
        def _():
            for cp in copies(nbb, nci, 1 - slot):
                cp.start()

        for cp in copies(b, ci, slot):
            cp.wait()
        attend(cbuf[slot].astype(BF16), krbuf[slot], kbuf[slot].astype(BF16), vbuf[slot].astype(BF16), None)
        return carry

    lax.fori_loop(0, nchunk, chunk, 0)

    key = lax.broadcasted_iota(jnp.int32, (NQ_MLA, NEW_PAD), 1)
    step = lax.broadcasted_iota(jnp.int32, (NQ_MLA, NEW_PAD), 0) % nstep
    keyd = lax.broadcasted_iota(jnp.int32, (NQ_DF, NEW_PAD), 1)
    stepd = lax.broadcasted_iota(jnp.int32, (NQ_DF, NEW_PAD), 0) % nstep
    attend(cn_ref[...], krn_ref[...], kdn_ref[...], vdn_ref[...], (key <= step, keyd <= stepd))

    o_abs = (acc1[...] / l1[...]).astype(BF16)
    full = jnp.dot(o_abs, wuv_ref[...], preferred_element_type=F32)
    head = lax.broadcasted_iota(jnp.int32, (NQ_MLA, MLA_V), 0) // nstep
    o = jnp.zeros((NQ_MLA, MLA_V), F32)
    for h in range(MLA_HEADS):
        o = jnp.where(head == h, full[:, h * MLA_V:(h + 1) * MLA_V], o)
    omla_ref[...] = o
    od = acc2[...] / l2[...]
    odf_ref[...] = _df_merge_norm(od[0:NQ_DF // 2], od[NQ_DF // 2:], (lq1, lk1, lq2, lk2), lamc_ref, sg_ref)


def _paged_attention(page_table, layer, qa, qr, qd, cn, krn, kdn, vdn, lw, caches, pages, nstep):
    nb, npages = page_table.shape
    nchunk = npages // pages
    lc = pages * PAGE_SIZE
    per_b = lambda a: pl.BlockSpec((None,) + a.shape[1:], lambda b, pt, ly: (b,) + (0,) * (a.ndim - 1))
    full = lambda a: pl.BlockSpec(a.shape, lambda b, pt, ly: (0,) * a.ndim)
    any_spec = pl.BlockSpec(memory_space=pl.ANY)
    batched = [qa, qr, qd, cn, krn, kdn, vdn]
    shared = [lw['wukt'], lw['wuv'], lw['df_lq1'], lw['df_lk1'], lw['df_lq2'], lw['df_lk2'], lw['lamc'],
              lw['df_subln_g']]
    ospec = pl.BlockSpec((None, NQ_MLA, LANES), lambda b, pt, ly: (b, 0, 0))
    grid_spec = pltpu.PrefetchScalarGridSpec(
        num_scalar_prefetch=2,
        grid=(nb,),
        in_specs=[per_b(a) for a in batched] + [full(a) for a in shared] + [any_spec] * 4,
        out_specs=[ospec, ospec],
        scratch_shapes=[pltpu.VMEM((2, lc, MLA_KV_RANK), F32), pltpu.VMEM((2, MLA_ROPE, lc), F32),
                        pltpu.VMEM((2, lc, DF_KDIM), F32), pltpu.VMEM((2, lc, DF_V), F32),
                        pltpu.SemaphoreType.DMA((4, 2)),
                        pltpu.VMEM((MLA_HEADS * MLA_NOPE + NQ_MLA, MLA_KV_RANK), BF16),
                        pltpu.VMEM((NQ_MLA, 1), F32), pltpu.VMEM((NQ_MLA, 1), F32),
                        pltpu.VMEM((NQ_MLA, MLA_KV_RANK), F32),
                        pltpu.VMEM((NQ_DF, 1), F32), pltpu.VMEM((NQ_DF, 1), F32),
                        pltpu.VMEM((NQ_DF, DF_V), F32)])
    return pl.pallas_call(
        functools.partial(_paged_kernel, pages=pages, nchunk=nchunk, nstep=nstep),
        grid_spec=grid_spec,
        out_shape=[jax.ShapeDtypeStruct((nb, NQ_MLA, MLA_V), F32), jax.ShapeDtypeStruct((nb, NQ_MLA, DF_V), F32)],
        compiler_params=_cparams(("arbitrary",)),
    )(page_table, layer, *batched, *shared, *caches)


def _out_kernel(x_ref, orw_ref, bonus_ref, omla_ref, odf_ref, g_ref, wout_ref, gng_ref, gnb_ref, lng_ref,
                xo_ref, h_ref):
    red, exp = _group_mats(RW_WIDTH, RW_HEAD_DIM, LANES)
    o = orw_ref[...]
    mean = _group_sum(o, red, exp) * (1.0 / RW_HEAD_DIM)
    d = o - mean
    var = _group_sum(d * d, red, exp) * (1.0 / RW_HEAD_DIM)
    y_rw = d * lax.rsqrt(var + GN_EPS) * gng_ref[...] + gnb_ref[...] + bonus_ref[...]
    g = g_ref[...]
    sg = g * jax.nn.sigmoid(g)
    a0, a1 = RW_WIDTH, RW_WIDTH + MLA_WIDTH
    acc = jnp.dot((y_rw * sg[:, 0:a0]).astype(BF16), wout_ref[0:a0, :], preferred_element_type=F32)
    acc += jnp.dot((omla_ref[...] * sg[:, a0:a1]).astype(BF16), wout_ref[a0:a1, :], preferred_element_type=F32)
    acc += jnp.dot((odf_ref[...] * sg[:, a1:]).astype(BF16), wout_ref[a1:, :], preferred_element_type=F32)
    xn = x_ref[...] + acc
    xo_ref[...] = xn
    ms = jnp.mean(xn * xn, axis=-1, keepdims=True)
    h_ref[...] = (xn * lax.rsqrt(ms + NORM_EPS) * lng_ref[...]).astype(BF16)


def _out_proj(x, o_rw, bonus, o_mla, o_df, gates, wout, gn_g, gn_b, ln_g_next, tm):
    rows = x.shape[0]
    row = lambda n: pl.BlockSpec((tm, n), lambda i: (i, 0))
    full = lambda a: pl.BlockSpec(a.shape, lambda i: (0,) * a.ndim)
    return pl.pallas_call(
        _out_kernel,
        grid=(rows // tm,),
        in_specs=[row(D_MODEL), row(RW_WIDTH), row(RW_WIDTH), row(MLA_WIDTH), row(DF_WIDTH), row(D_MIX),
                  full(wout), full(gn_g), full(gn_b), full(ln_g_next)],
        out_specs=[row(D_MODEL), row(D_MODEL)],
        out_shape=[jax.ShapeDtypeStruct((rows, D_MODEL), F32), jax.ShapeDtypeStruct((rows, D_MODEL), BF16)],
        compiler_params=_cparams(("parallel",)),
    )(x, o_rw, bonus, o_mla, o_df, gates, wout, gn_g, gn_b, ln_g_next)


def _prep_weights(P):
    depth = P['w_in'].shape[0]
    w_in = P['w_in']
    kr0 = OFF_MLA + MLA_Q_RANK + MLA_KV_RANK
    half = MLA_ROPE // 2
    kr_swapped = jnp.concatenate([w_in[:, :, kr0 + half:kr0 + MLA_ROPE], w_in[:, :, kr0:kr0 + half]], axis=-1)
    W = {
        'w_rw': w_in[:, :, 0:OFF_RW_G].astype(BF16),
        'w_gate': jnp.concatenate([w_in[:, :, OFF_RW_G:OFF_MLA], w_in[:, :, OFF_MLA_G:OFF_DF],
                                   w_in[:, :, OFF_DF_G:N_IN]], axis=-1).astype(BF16),
        'w_mla': jnp.concatenate([w_in[:, :, OFF_MLA:OFF_MLA_G], kr_swapped], axis=-1).astype(BF16),
        'w_df': w_in[:, :, OFF_DF:OFF_DF_G].astype(BF16),
        'w_out': P['w_out'].astype(BF16),
    }
    zl = jnp.zeros((depth, RW_LORA, RW_WIDTH), F32)
    W['wup'] = jnp.concatenate([P['rw_w_up'], zl], axis=1).astype(BF16)
    W['aup'] = jnp.concatenate([zl, P['rw_a_up']], axis=1).astype(BF16)
    uq = P['mla_w_uq'].reshape(depth, MLA_Q_RANK, MLA_HEADS, MLA_QK)
    nope, rope = uq[..., :MLA_NOPE], uq[..., MLA_NOPE:]
    rope_sw = jnp.concatenate([rope[..., half:], rope[..., :half]], axis=-1)
    z64 = jnp.zeros(rope.shape, F32)
    z128 = jnp.zeros(nope.shape, F32)
    W['wuq'] = jnp.concatenate([nope, rope, z64], axis=-1).reshape(depth, MLA_Q_RANK, -1).astype(BF16)
    W['wuqs'] = jnp.concatenate([z128, rope_sw, z64], axis=-1).reshape(depth, MLA_Q_RANK, -1).astype(BF16)
    ukv = P['mla_w_ukv'].reshape(depth, MLA_KV_RANK, MLA_HEADS, MLA_NOPE + MLA_V)
    wuk = ukv[..., :MLA_NOPE].reshape(depth, MLA_KV_RANK, -1)
    W['wuk'] = wuk.astype(BF16)
    W['wukt'] = jnp.swapaxes(wuk, 1, 2).astype(BF16)
    W['wuv'] = ukv[..., MLA_NOPE:].reshape(depth, MLA_KV_RANK, -1).astype(BF16)
    pad = jnp.zeros((depth, MLA_QPAD - MLA_QK), F32)
    W['mla_qg'] = jnp.concatenate([P['mla_qn_g'], pad], axis=-1)
    W['mla_kg'] = jnp.concatenate([P['mla_kn_g'], pad], axis=-1)
    W['df_qg'] = jnp.tile(P['df_qn_g'], (1, DF_NQ // DF_QK))
    W['df_kg'] = jnp.tile(P['df_kn_g'], (1, DF_KDIM // DF_QK))
    W['rw_r_k'] = P['rw_r_k'].reshape(depth, RW_WIDTH)
    lam_init = jnp.asarray([0.8 - 0.6 * math.exp(-0.3 * i) for i in range(depth)], F32)
    W['lamc'] = jnp.broadcast_to(lam_init[:, None], (depth, LANES))
    for name in ('ln_g', 'rw_mu', 'rw_w0', 'rw_a0', 'rw_k_k', 'rw_k_a', 'rw_gn_g', 'rw_gn_b', 'mla_q_norm_g',
                 'mla_kv_norm_g', 'df_lq1', 'df_lk1', 'df_lq2', 'df_lk2', 'df_subln_g'):
        W[name] = P[name]
    return W


def _layer_weights(W, i):
    out = {}
    for name, a in W.items():
        out[name] = a[i] if a.ndim == 3 else a[i][None, :]
    return out


def _rope_tables(pos):
    half = MLA_ROPE // 2
    inv = ROPE_THETA ** (-jnp.arange(half, dtype=F32) / half)
    ang = pos.astype(F32)[:, None] * inv[None, :]
    cos, sin = jnp.cos(ang), jnp.sin(ang)
    n = pos.shape[0]
    cc = jnp.concatenate([cos, cos], axis=-1)
    ss = jnp.concatenate([-sin, sin], axis=-1)
    cs = jnp.concatenate([jnp.ones((n, MLA_NOPE), F32), cc, jnp.zeros((n, MLA_QPAD - MLA_QK), F32)], axis=-1)
    sn = jnp.concatenate([jnp.zeros((n, MLA_NOPE), F32), ss, jnp.zeros((n, MLA_QPAD - MLA_QK), F32)], axis=-1)
    ckr = jnp.concatenate([cc, ss], axis=-1)
    return cs, sn, ckr


def _pick(n, prefs):
    for p in prefs:
        if n % p == 0:
            return p
    return n


def _trunk(x, W, past, depth):
    nb, t, _ = x.shape
    rows = nb * t
    prompt = past is None
    tm = _pick(rows, (1024, 512, 256, 128))
    tm_small = _pick(rows, (512, 256, 128))
    tm_out = _pick(rows, (256, 128))
    tm_rw = _pick(rows, (256, 128))
    if prompt:
        pos = jnp.arange(t)
        tabs = _rope_tables(pos)
        tq = _pick(t, (512, 256, 128))
    else:
        npages = past['page_table'].shape[1]
        pos = npages * PAGE_SIZE + jnp.arange(t)
        tabs = tuple(jnp.tile(a, (nb, 1)) for a in _rope_tables(pos))
        pages = _pick(npages, (32, 16, 8, 4, 2, 1))
        if npages // pages < 2 or (npages // pages) % 2:
            pages = max(1, pages // 2)
    xf = x.reshape(rows, D_MODEL)
    h = _norm_rows(xf, W['ln_g'][0][None, :], tm_small)
    new = {k: [] for k in ('mla_latent', 'mla_krope', 'diff_k', 'diff_v', 'rwkv_state', 'rwkv_shift')}
    for i in range(depth):
        lw = _layer_weights(W, i)
        z_rw = _matmul(h, lw['w_rw'], tm, _pick(RW_SHIFT_DIM, (640,)))
        gates = _matmul(h, lw['w_gate'], tm, _pick(D_MIX, (1024,)))
        z_mla = _matmul(h, lw['w_mla'], tm, lw['w_mla'].shape[1])
        z_df = _matmul(h, lw['w_df'], tm, DF_IN)

        z3 = z_rw.reshape(nb, t, RW_SHIFT_DIM)
        shift0 = jnp.zeros((nb, RW_SHIFT_DIM), F32) if prompt else past['rwkv_shift'][i]
        rw_args = (lw['rw_mu'], lw['rw_w0'], lw['wup'], lw['rw_a0'], lw['aup'],
                   lw['rw_k_k'], lw['rw_k_a'], lw['rw_r_k'])
        if prompt:
            *folded, pc, bonus = _rw_prep_chunk(z_rw, shift0, *rw_args, tm_rw, t)
            rp, oi, q, ds = _rw_chunk(folded)
            o_rw, st = _rw_state(rp, oi, pc, q, ds, nb, t)
        else:
            prev = jnp.concatenate([shift0[:, None, :], z3[:, :-1]], axis=1).reshape(rows, RW_SHIFT_DIM)
            r, w, kmod, v, kk, kka, bonus = _rw_prep(z_rw, prev, *rw_args, tm_rw)
            heads = lambda a: a.reshape(rows, RW_HEADS, RW_HEAD_DIM)
            cols = [heads(a).transpose(1, 2, 0) for a in (kk, r, v)]
            rows_ = [heads(a).transpose(1, 0, 2) for a in (kka, kmod, w)]
            o_t, st = _rw_scan_sample(cols, rows_, past['rwkv_state'][i], t)
            o_rw = o_t.transpose(2, 0, 1).reshape(rows, RW_WIDTH)
        new['rwkv_state'].append(st)
        new['rwkv_shift'].append(z3[:, -1])

        qd, kd, vd, kdb, vdb = _df_prep(z_df, lw['df_qg'], lw['df_kg'], tm_small)
        lam_args = [lw['df_lq1'], lw['df_lk1'], lw['df_lq2'], lw['df_lk2'], lw['lamc'], lw['df_subln_g']]
        if prompt:
            q, c, kr, k, vm = _mla_prep(z_mla, tabs, lw, tm_small, True)
            o_mla = _mla_flash(q.reshape(nb, t, -1), k.reshape(nb, t, -1), vm.reshape(nb, t, -1), tq)
            o_df = _df_flash(qd.reshape(nb, t, -1), kdb.reshape(nb, t, -1), vdb.reshape(nb, t, -1), lam_args, tq)
            o_mla = o_mla.reshape(rows, MLA_WIDTH)
            o_df = o_df.reshape(rows, DF_WIDTH)
        else:
            qa, qr, c, kr = _mla_prep(z_mla, tabs, lw, tm_small, False)
            qa = qa.reshape(nb, t, MLA_HEADS, MLA_KV_RANK).transpose(0, 2, 1, 3).reshape(nb, -1, MLA_KV_RANK)
            qr = qr.reshape(nb, t, MLA_HEADS, LANES)[..., :MLA_ROPE].transpose(0, 2, 1, 3)
            qr = qr.reshape(nb, -1, MLA_ROPE)
            q5 = qd.reshape(nb, t, DF_HEADS, 2, DF_QK).transpose(0, 3, 2, 1, 4)
            zq = jnp.zeros_like(q5[:, 0])
            qdm = jnp.stack([jnp.concatenate([q5[:, 0], zq], axis=-1),
                             jnp.concatenate([zq, q5[:, 1]], axis=-1)], axis=1).reshape(nb, -1, 2 * DF_QK)
            padk = lambda a: jnp.pad(a.reshape(nb, t, -1), ((0, 0), (0, NEW_PAD - t), (0, 0)))
            layer = jnp.full((1,), i, jnp.int32)
            caches = (past['mla_latent'], past['mla_krope_t'], past['diff_k'], past['diff_v'])
            krn = jnp.pad(jnp.swapaxes(kr.reshape(nb, t, MLA_ROPE), 1, 2), ((0, 0), (0, 0), (0, NEW_PAD - t)))
            o_mla, o_df = _paged_attention(
                past['page_table'], layer, qa, qr, qdm, padk(c).astype(BF16), krn, padk(kdb), padk(vdb),
                lw, caches, pages, t)
            o_mla = o_mla.reshape(nb, MLA_HEADS, t, MLA_V).transpose(0, 2, 1, 3).reshape(rows, MLA_WIDTH)
            o_df = o_df.reshape(nb, DF_HEADS, t, DF_V).transpose(0, 2, 1, 3).reshape(rows, DF_WIDTH)
        new['mla_latent'].append(c.reshape(nb, t, MLA_KV_RANK))
        new['mla_krope'].append(kr.reshape(nb, t, MLA_ROPE))
        new['diff_k'].append(kd.reshape(nb, t, DF_KDIM))
        new['diff_v'].append(vd.reshape(nb, t, DF_V))

        ln_next = W['ln_g'][min(i + 1, depth - 1)][None, :]
        xf, h = _out_proj(xf, o_rw, bonus, o_mla, o_df, gates, lw['w_out'], lw['rw_gn_g'], lw['rw_gn_b'],
                          ln_next, tm_out)
    return xf.reshape(nb, t, D_MODEL), {k: jnp.stack(vs) for k, vs in new.items()}


def kernel(x_prompt, x_sample, cache_mla_latent, cache_mla_krope, cache_diff_k, cache_diff_v, state_rwkv, state_rwkv_shift, page_table, ln_g, w_in, w_out, rw_mu, rw_w0, rw_w_up, rw_a0, rw_a_up, rw_k_k, rw_k_a, rw_r_k, rw_gn_g, rw_gn_b, mla_q_norm_g, mla_w_uq, mla_kv_norm_g, mla_w_ukv, mla_qn_g, mla_kn_g, df_qn_g, df_kn_g, df_lq1, df_lk1, df_lq2, df_lk2, df_subln_g):
    P = dict(ln_g=ln_g, w_in=w_in, w_out=w_out, rw_mu=rw_mu, rw_w0=rw_w0, rw_w_up=rw_w_up,
             rw_a0=rw_a0, rw_a_up=rw_a_up, rw_k_k=rw_k_k, rw_k_a=rw_k_a, rw_r_k=rw_r_k,
             rw_gn_g=rw_gn_g, rw_gn_b=rw_gn_b, mla_q_norm_g=mla_q_norm_g, mla_w_uq=mla_w_uq,
             mla_kv_norm_g=mla_kv_norm_g, mla_w_ukv=mla_w_ukv, mla_qn_g=mla_qn_g,
             mla_kn_g=mla_kn_g, df_qn_g=df_qn_g, df_kn_g=df_kn_g, df_lq1=df_lq1, df_lk1=df_lk1,
             df_lq2=df_lq2, df_lk2=df_lk2, df_subln_g=df_subln_g)
    depth = w_in.shape[0]
    W = _prep_weights(P)
    y_p, sp = _trunk(x_prompt, W, None, depth)
    past = dict(mla_latent=cache_mla_latent, mla_krope_t=jnp.swapaxes(cache_mla_krope, 2, 3), diff_k=cache_diff_k,
                diff_v=cache_diff_v, rwkv_state=state_rwkv, rwkv_shift=state_rwkv_shift,
                page_table=page_table)
    y_s, ss = _trunk(x_sample, W, past, depth)
    keys = ('mla_latent', 'mla_krope', 'diff_k', 'diff_v', 'rwkv_state', 'rwkv_shift')
    return (y_p, y_s) + tuple(sp[k] for k in keys) + tuple(ss[k] for k in keys)
```
